```python
import jax, jax.numpy as jnp
from jax import lax
import numpy as np

D_MODEL = 2048
BATCH = 8
SEQ = 2048
DEPTH = 1

CTX_LEN = 256
GRID_W = 64
Q_BLOCK = 128
ROPE_THETA = 10000.0
NORM_EPS = 1e-6

MLA_HEADS = 8
MLA_Q_LORA = 768
MLA_KV_LORA = 512
MLA_NOPE = 128
MLA_ROPE = 64
MLA_V = 128
GQA_HEADS = 8
GQA_KV_HEADS = 2
GQA_HEAD_DIM = 128
D_FF = 5632
CONV_W = 3
N_BRANCH = 2

KV_COLS = MLA_KV_LORA + MLA_ROPE + 2 * GQA_KV_HEADS * GQA_HEAD_DIM
Q_COLS = MLA_Q_LORA + GQA_HEADS * GQA_HEAD_DIM
GATE_COLS = N_BRANCH * D_MODEL
IN_COLS = KV_COLS + Q_COLS + GATE_COLS
KV_SPLITS = [MLA_KV_LORA, MLA_KV_LORA + MLA_ROPE, MLA_KV_LORA + MLA_ROPE + GQA_KV_HEADS * GQA_HEAD_DIM]

kernel_name = "hybrid_mla_gqa_convffn_dit_prefix"


def rms_norm(x, g):
    xf = x.astype(jnp.float32)
    y = xf * lax.rsqrt(jnp.mean(xf * xf, axis=-1, keepdims=True) + NORM_EPS)
    return (y * g.astype(jnp.float32)).astype(x.dtype)


def modulate(h, shift, scale):
    return h * (1 + scale) + shift


def ada_terms(cond, w_ada, b_ada):
    return jnp.split(jax.nn.silu(cond) @ w_ada + b_ada, 6, axis=-1)


def grid_rope_tables(n_rows, rot_dim):
    row = jnp.repeat(jnp.arange(n_rows, dtype=jnp.float32), GRID_W)
    col = jnp.tile(jnp.arange(GRID_W, dtype=jnp.float32), n_rows)
    half = rot_dim // 2
    inv_freq = ROPE_THETA ** (-jnp.arange(0, half, 2, dtype=jnp.float32) / half)
    ang = jnp.concatenate([row[:, None] * inv_freq, col[:, None] * inv_freq], axis=-1)
    return jnp.cos(ang), jnp.sin(ang)


def apply_grid_rope(x, cos, sin):
    b, t, h, r = x.shape
    q = r // 4
    xs = x.reshape(b, t, h, 2, 2, q)
    x1, x2 = xs[..., 0, :], xs[..., 1, :]
    c = cos.reshape(t, 1, 2, q).astype(x.dtype)
    s = sin.reshape(t, 1, 2, q).astype(x.dtype)
    out = jnp.stack([x1 * c - x2 * s, x1 * s + x2 * c], axis=-2)
    return out.reshape(b, t, h, r)


def block_attention(q, k, v):
    b, tq, hk, g, dk = q.shape
    dv = v.shape[-1]
    scale = dk ** -0.5
    kf = k.astype(jnp.float32)
    qb = jnp.moveaxis(q.reshape(b, tq // Q_BLOCK, Q_BLOCK, hk, g, dk), 1, 0)

    def one_block(q_blk):
        s = jnp.einsum("bqhgd,bkhd->bhgqk", q_blk.astype(jnp.float32), kf) * scale
        p = jax.nn.softmax(s, axis=-1)
        return jnp.einsum("bhgqk,bkhd->bqhgd", p.astype(v.dtype), v)

    o = lax.map(one_block, qb)
    return jnp.moveaxis(o, 0, 1).reshape(b, tq, hk * g * dv)


def mixer_keys(kv, p, rope):
    b, t, _ = kv.shape
    c_kv, k_pe, k_b, v_b = jnp.split(kv, KV_SPLITS, axis=-1)
    kv_up = (rms_norm(c_kv, p["mla_kv_norm_g"]) @ p["w_kv_up"]).reshape(b, t, MLA_HEADS, MLA_NOPE + MLA_V)
    k_nope, v_a = jnp.split(kv_up, [MLA_NOPE], axis=-1)
    k_pe = k_pe.reshape(b, t, 1, MLA_ROPE)
    k_b = rms_norm(k_b.reshape(b, t, GQA_KV_HEADS, GQA_HEAD_DIM), p["gqa_k_norm_g"])
    v_b = v_b.reshape(b, t, GQA_KV_HEADS, GQA_HEAD_DIM)
    if rope is not None:
        cos_a, sin_a, cos_b, sin_b = rope
        k_pe = apply_grid_rope(k_pe, cos_a, sin_a)
        k_b = apply_grid_rope(k_b, cos_b, sin_b)
    k_a = jnp.concatenate([k_nope, jnp.broadcast_to(k_pe, (b, t, MLA_HEADS, MLA_ROPE))], axis=-1)
    return (k_a, v_a, k_b, v_b)


def mixer_queries(qp, p, rope):
    b, t, _ = qp.shape
    c_q, q_b = jnp.split(qp, [MLA_Q_LORA], axis=-1)
    q_a = (rms_norm(c_q, p["mla_q_norm_g"]) @ p["w_q_up"]).reshape(b, t, MLA_HEADS, MLA_NOPE + MLA_ROPE)
    q_nope, q_pe = jnp.split(q_a, [MLA_NOPE], axis=-1)
    q_b = rms_norm(q_b.reshape(b, t, GQA_HEADS, GQA_HEAD_DIM), p["gqa_q_norm_g"])
    if rope is not None:
        cos_a, sin_a, cos_b, sin_b = rope
        q_pe = apply_grid_rope(q_pe, cos_a, sin_a)
        q_b = apply_grid_rope(q_b, cos_b, sin_b)
    q_a = jnp.concatenate([q_nope, q_pe], axis=-1)[:, :, :, None, :]
    q_b = q_b.reshape(b, t, GQA_KV_HEADS, GQA_HEADS // GQA_KV_HEADS, GQA_HEAD_DIM)
    return q_a, q_b


def attend_and_merge(proj, keys, p, rope):
    q_a, q_b = mixer_queries(proj[..., KV_COLS:KV_COLS + Q_COLS], p, rope)
    g_a, g_b = jnp.split(jax.nn.sigmoid(proj[..., KV_COLS + Q_COLS:]), N_BRANCH, axis=-1)
    k_a, v_a, k_b, v_b = keys
    o_a = block_attention(q_a, k_a, v_a)
    o_b = block_attention(q_b, k_b, v_b)
    merged = g_a * (o_a @ p["w_br_a"]) + g_b * (o_b @ p["w_br_b"])
    return merged @ p["w_out"]


def conv_ffn(z, p):
    t = z.shape[1]
    u = z @ p["w_up"]
    pad = CONV_W // 2
    up = jnp.pad(u, ((0, 0), (pad, pad), (0, 0)))
    uc = p["conv_b"] + sum(p["conv_w"][j] * up[:, j:j + t] for j in range(CONV_W))
    a, bb = jnp.split(uc, 2, axis=-1)
    return (jax.nn.silu(a) * bb) @ p["w_down"]


def setup_inputs(seed: int = 0) -> dict:
    key = jax.random.key(seed)
    ks = jax.random.split(key, 24)
    f32 = jnp.float32

    def nrm(k, shape, scale):
        return jax.random.normal(k, shape, f32) * scale

    def gain(k, shape):
        return 1.0 + 0.01 * jax.random.normal(k, shape, f32)

    L, D = DEPTH, D_MODEL
    return {
        "x": nrm(ks[0], (BATCH, SEQ, D), 1.0),
        "c": nrm(ks[1], (BATCH, D), 1.0),
        "ctx": nrm(ks[2], (BATCH, CTX_LEN, D), 1.0),
        "c_ctx": nrm(ks[3], (D,), 0.5),
        "w_ada": nrm(ks[4], (L, D, 6 * D), 0.5 * D ** -0.5),
        "b_ada": nrm(ks[5], (L, 6 * D), 0.01),
        "norm1_g": gain(ks[6], (L, D)),
        "w_in": nrm(ks[7], (L, D, IN_COLS), D ** -0.5),
        "mla_q_norm_g": gain(ks[8], (L, MLA_Q_LORA)),
        "w_q_up": nrm(ks[9], (L, MLA_Q_LORA, MLA_HEADS * (MLA_NOPE + MLA_ROPE)), MLA_Q_LORA ** -0.5),
        "mla_kv_norm_g": gain(ks[10], (L, MLA_KV_LORA)),
        "w_kv_up": nrm(ks[11], (L, MLA_KV_LORA, MLA_HEADS * (MLA_NOPE + MLA_V)), MLA_KV_LORA ** -0.5),
        "gqa_q_norm_g": gain(ks[12], (L, GQA_HEAD_DIM)),
        "gqa_k_norm_g": gain(ks[13], (L, GQA_HEAD_DIM)),
        "w_br_a": nrm(ks[14], (L, MLA_HEADS * MLA_V, D), (MLA_HEADS * MLA_V) ** -0.5),
        "w_br_b": nrm(ks[15], (L, GQA_HEADS * GQA_HEAD_DIM, D), (GQA_HEADS * GQA_HEAD_DIM) ** -0.5),
        "w_out": nrm(ks[16], (L, D, D), D ** -0.5),
        "norm2_g": gain(ks[17], (L, D)),
        "w_up": nrm(ks[18], (L, D, 2 * D_FF), D ** -0.5),
        "conv_w": nrm(ks[19], (L, CONV_W, 2 * D_FF), CONV_W ** -0.5),
        "conv_b": nrm(ks[20], (L, 2 * D_FF), 0.01),
        "w_down": nrm(ks[21], (L, D_FF, D), D_FF ** -0.5),
        "final_norm_g": gain(ks[22], (D,)),
    }


def reference(x, c, ctx, c_ctx, w_ada, b_ada, norm1_g, w_in, mla_q_norm_g, w_q_up, mla_kv_norm_g,
              w_kv_up, gqa_q_norm_g, gqa_k_norm_g, w_br_a, w_br_b, w_out, norm2_g, w_up, conv_w,
              conv_b, w_down, final_norm_g):
    n_lat = x.shape[1]
    ROWS = n_lat // GRID_W
    rope = (*grid_rope_tables(ROWS, MLA_ROPE), *grid_rope_tables(ROWS, GQA_HEAD_DIM))
    cond_lat = c[:, None, :]
    cond_ctx = c_ctx[None, None, :]

    for l in range(DEPTH):
        p = {
            "w_in": w_in[l], "mla_q_norm_g": mla_q_norm_g[l], "w_q_up": w_q_up[l],
            "mla_kv_norm_g": mla_kv_norm_g[l], "w_kv_up": w_kv_up[l],
            "gqa_q_norm_g": gqa_q_norm_g[l], "gqa_k_norm_g": gqa_k_norm_g[l],
            "w_br_a": w_br_a[l], "w_br_b": w_br_b[l], "w_out": w_out[l],
            "w_up": w_up[l], "conv_w": conv_w[l], "conv_b": conv_b[l], "w_down": w_down[l],
        }
        last = l == DEPTH - 1
        sh1, sc1, g1, sh2, sc2, g2 = ada_terms(cond_lat, w_ada[l], b_ada[l])
        ctx_terms = ada_terms(cond_ctx, w_ada[l], b_ada[l])

        z_ctx = modulate(rms_norm(ctx, norm1_g[l]), ctx_terms[0], ctx_terms[1])
        ctx_proj = z_ctx @ (p["w_in"][:, :KV_COLS] if last else p["w_in"])
        ctx_keys = mixer_keys(ctx_proj[..., :KV_COLS], p, None)

        z_lat = modulate(rms_norm(x, norm1_g[l]), sh1, sc1)
        lat_proj = z_lat @ p["w_in"]
        lat_keys = mixer_keys(lat_proj[..., :KV_COLS], p, rope)
        keys = tuple(jnp.concatenate([ck, lk], axis=1) for ck, lk in zip(ctx_keys, lat_keys))
        x = x + g1 * attend_and_merge(lat_proj, keys, p, rope)
        x = x + g2 * conv_ffn(modulate(rms_norm(x, norm2_g[l]), sh2, sc2), p)

        if not last:
            ctx = ctx + ctx_terms[2] * attend_and_merge(ctx_proj, ctx_keys, p, None)
            z2 = modulate(rms_norm(ctx, norm2_g[l]), ctx_terms[3], ctx_terms[4])
            ctx = ctx + ctx_terms[5] * conv_ffn(z2, p)

    return rms_norm(x, final_norm_g)
```

```python
import functools

import jax
import jax.numpy as jnp
from jax import lax
from jax.experimental import pallas as pl
from jax.experimental.pallas import tpu as pltpu

GRID_W = 64
ROPE_THETA = 10000.0
NORM_EPS = 1e-6
MLA_HEADS = 8
MLA_Q_LORA = 768
MLA_KV_LORA = 512
MLA_NOPE = 128
MLA_ROPE = 64
MLA_V = 128
GQA_HEADS = 8
GQA_KV_HEADS = 2
GQA_HEAD_DIM = 128
CONV_W = 3
N_BRANCH = 2

LANES = 128
BF16_SUBLANES = 16
VMEM_LIMIT_BYTES = 56 * 1024 * 1024

TM_KVQ = 512
TM_MERGE = 512
TC_MERGE = 512
TM_FFN = 512
TF_FFN = 512
TQ_ATTN = 256
TN_ADA = 1536
ADA_ROWS = 16

F32 = jnp.float32
BF16 = jnp.bfloat16

_C_KV = 0
_K_B = _C_KV + MLA_KV_LORA
_V_B = _K_B + GQA_KV_HEADS * GQA_HEAD_DIM
_K_PE = _V_B + GQA_KV_HEADS * GQA_HEAD_DIM
_KV_END = _K_PE + LANES
_C_Q = _KV_END
_Q_B = _C_Q + MLA_Q_LORA
_Q_END = _Q_B + GQA_HEADS * GQA_HEAD_DIM
QA_HEAD_W = 2 * LANES


def _params(n_axes):
    return pltpu.CompilerParams(
        dimension_semantics=("arbitrary",) * n_axes,
        vmem_limit_bytes=VMEM_LIMIT_BYTES,
    )


def _resident(shape):
    zeros = (0,) * len(shape)
    return pl.BlockSpec(shape, lambda *_: zeros, pipeline_mode=pl.Buffered(1))


def _rms(x, g):
    return x * lax.rsqrt(jnp.mean(x * x, axis=-1, keepdims=True) + NORM_EPS) * g


def _rope(t, cos, sin_signed, quarter):
    lane = lax.broadcasted_iota(jnp.int32, t.shape, 1)
    first = (lane % (2 * quarter)) < quarter
    partner = jnp.where(first, pltpu.roll(t, LANES - quarter, 1), pltpu.roll(t, quarter, 1))
    return t * cos + partner * sin_signed


def _sigmoid(x):
    return 1.0 / (1.0 + jnp.exp(-x))


def _ada_kernel(c_ref, w_ref, b_ref, o_ref):
    c = c_ref[...]
    s = (c * _sigmoid(c)).astype(BF16)
    o_ref[...] = jnp.dot(s, w_ref[...].astype(BF16), preferred_element_type=F32) + b_ref[...]


def _ada_call(cond, w_ada, b_ada):
    d, n = w_ada.shape
    return pl.pallas_call(
        _ada_kernel,
        out_shape=jax.ShapeDtypeStruct((ADA_ROWS, n), F32),
        grid=(n // TN_ADA,),
        in_specs=[
            pl.BlockSpec((ADA_ROWS, d), lambda j: (0, 0)),
            pl.BlockSpec((d, TN_ADA), lambda j: (0, j)),
            pl.BlockSpec((1, TN_ADA), lambda j: (0, j)),
        ],
        out_specs=pl.BlockSpec((ADA_ROWS, TN_ADA), lambda j: (0, j)),
        compiler_params=_params(1),
        name="ada",
    )(cond, w_ada, b_ada)


def _kvq_kernel(*refs, with_q, with_rope):
    it = iter(refs)
    x_ref, mod_ref, n1g_ref, w_ref, gkv_ref, wkv_ref, gkb_ref = (next(it) for _ in range(7))
    if with_q:
        gq_ref, wq_ref, gqb_ref = (next(it) for _ in range(3))
    if with_rope:
        ca_ref, sa_ref, cb_ref, sb_ref = (next(it) for _ in range(4))
    kn_ref, va_ref, kpe_ref, kb_ref, vb_ref = (next(it) for _ in range(5))
    if with_q:
        qa_ref, qb_ref = (next(it) for _ in range(2))

    mod = mod_ref[0]
    z = (_rms(x_ref[0], n1g_ref[...]) * (1.0 + mod[1:2]) + mod[0:1]).astype(BF16)
    proj = jnp.dot(z, w_ref[...], preferred_element_type=F32)

    if with_rope:
        ca, sa, cb, sb = ca_ref[...], sa_ref[...], cb_ref[...], sb_ref[...]

    ckv = _rms(proj[:, _C_KV:_K_B], gkv_ref[...]).astype(BF16)
    kv_up = jnp.dot(ckv, wkv_ref[...], preferred_element_type=F32)
    hn = MLA_HEADS * MLA_NOPE
    kn_ref[0] = kv_up[:, :hn].astype(BF16)
    va_ref[0] = kv_up[:, hn:].astype(BF16)

    kpe = proj[:, _K_PE:_KV_END]
    if with_rope:
        kpe = _rope(kpe, ca, sa, MLA_ROPE // 4)
    kpe_ref[0] = kpe.astype(BF16)

    for h in range(GQA_KV_HEADS):
        lo = _K_B + h * GQA_HEAD_DIM
        t = _rms(proj[:, lo:lo + GQA_HEAD_DIM], gkb_ref[...])
        if with_rope:
            t = _rope(t, cb, sb, GQA_HEAD_DIM // 4)
        kb_ref[0, :, h * GQA_HEAD_DIM:(h + 1) * GQA_HEAD_DIM] = t.astype(BF16)
    vb_ref[0] = proj[:, _V_B:_K_PE].astype(BF16)

    if with_q:
        scale_a = float(MLA_NOPE + MLA_ROPE) ** -0.5
        scale_b = float(GQA_HEAD_DIM) ** -0.5
        cq = _rms(proj[:, _C_Q:_Q_B], gq_ref[...]).astype(BF16)
        q_up = jnp.dot(cq, wq_ref[...], preferred_element_type=F32)
        for h in range(MLA_HEADS):
            lo = h * QA_HEAD_W
            qa_ref[0, :, lo:lo + LANES] = (q_up[:, lo:lo + LANES] * scale_a).astype(BF16)
            pe = q_up[:, lo + LANES:lo + QA_HEAD_W]
            if with_rope:
                pe = _rope(pe, ca, sa, MLA_ROPE // 4)
            qa_ref[0, :, lo + LANES:lo + QA_HEAD_W] = (pe * scale_a).astype(BF16)
        for h in range(GQA_HEADS):
            lo = _Q_B + h * GQA_HEAD_DIM
            t = _rms(proj[:, lo:lo + GQA_HEAD_DIM], gqb_ref[...])
            if with_rope:
                t = _rope(t, cb, sb, GQA_HEAD_DIM // 4)
            qb_ref[0, :, h * GQA_HEAD_DIM:(h + 1) * GQA_HEAD_DIM] = (t * scale_b).astype(BF16)


def _kvq_call(x, mod3, mod_row, n1g, w_small, gkv, wkv, gkb, q_parts, rope_tabs, tm):
    b, t, d = x.shape
    with_q = q_parts is not None
    with_rope = rope_tabs is not None
    nt = t // tm
    cols = w_small.shape[1]

    in_specs = [
        pl.BlockSpec((1, tm, d), lambda bi, i: (bi, i, 0)),
        pl.BlockSpec((1, 6, d), lambda bi, i: (mod_row(bi), 0, 0)),
        _resident((1, d)),
        _resident((d, cols)),
        _resident(gkv.shape),
        _resident(wkv.shape),
        _resident(gkb.shape),
    ]
    args = [x, mod3, n1g, w_small, gkv, wkv, gkb]
    if with_q:
        gq, wq, gqb = q_parts
        in_specs += [_resident(gq.shape), _resident(wq.shape), _resident(gqb.shape)]
        args += [gq, wq, gqb]
    if with_rope:
        in_specs += [pl.BlockSpec((tm, LANES), lambda bi, i: (i, 0))] * 4
        args += list(rope_tabs)

    def tok(width):
        return pl.BlockSpec((1, tm, width), lambda bi, i: (bi, i, 0))

    widths = [MLA_HEADS * MLA_NOPE, MLA_HEADS * MLA_V, LANES,
              GQA_KV_HEADS * GQA_HEAD_DIM, GQA_KV_HEADS * GQA_HEAD_DIM]
    if with_q:
        widths += [MLA_HEADS * QA_HEAD_W, GQA_HEADS * GQA_HEAD_DIM]
    return pl.pallas_call(
        functools.partial(_kvq_kernel, with_q=with_q, with_rope=with_rope),
        out_shape=[jax.ShapeDtypeStruct((b, t, w), BF16) for w in widths],
        grid=(b, nt),
        in_specs=in_specs,
        out_specs=[tok(w) for w in widths],
        compiler_params=_params(2),
        name="kvq_lat" if with_q else "kvq_ctx",
    )(*args)


def _softmax_pv(q, k_all, v_lat, v_ctx, n_lat):
    s = lax.dot_general(q, k_all, (((1,), (1,)), ((), ())), preferred_element_type=F32)
    m = jnp.max(s, axis=-1, keepdims=True)
    p = jnp.exp(s - m)
    l = jnp.sum(p, axis=-1, keepdims=True)
    pb = p.astype(BF16)
    o = jnp.dot(pb[:, :n_lat], v_lat, preferred_element_type=F32)
    o = o + jnp.dot(pb[:, n_lat:], v_ctx, preferred_element_type=F32)
    return o / l


def _attn_a_kernel(q_ref, knl_ref, kpl_ref, val_ref, knc_ref, kpc_ref, vac_ref, o_ref, k_scr,
                   *, tq):
    n_lat = knl_ref.shape[1]
    k_scr[:n_lat, :LANES] = knl_ref[0]
    k_scr[:n_lat, LANES:] = kpl_ref[0]
    k_scr[n_lat:, :LANES] = knc_ref[0]
    k_scr[n_lat:, LANES:] = kpc_ref[0]

    def chunk(c, carry):
        rows = pl.ds(pl.multiple_of(c * tq, tq), tq)
        o = _softmax_pv(q_ref[0, rows, :], k_scr[...], val_ref[0], vac_ref[0], n_lat)
        o_ref[0, rows, :] = o.astype(BF16)
        return carry

    lax.fori_loop(0, q_ref.shape[1] // tq, chunk, 0)


def _attn_a_call(qa, kn_l, kpe_l, va_l, kn_c, kpe_c, va_c):
    b, s, _ = qa.shape
    c = kn_c.shape[1]
    tq = min(TQ_ATTN, s)

    def head(rows, width):
        return pl.BlockSpec((1, rows, width), lambda bi, h: (bi, 0, h))

    def shared(rows):
        return pl.BlockSpec((1, rows, LANES), lambda bi, h: (bi, 0, 0))

    return pl.pallas_call(
        functools.partial(_attn_a_kernel, tq=tq),
        out_shape=jax.ShapeDtypeStruct((b, s, MLA_HEADS * MLA_V), BF16),
        grid=(b, MLA_HEADS),
        in_specs=[head(s, QA_HEAD_W), head(s, MLA_NOPE), shared(s), head(s, MLA_V),
                  head(c, MLA_NOPE), shared(c), head(c, MLA_V)],
        out_specs=head(s, MLA_V),
        scratch_shapes=[pltpu.VMEM((s + c, QA_HEAD_W), BF16)],
        compiler_params=_params(2),
        name="attn_a",
    )(qa, kn_l, kpe_l, va_l, kn_c, kpe_c, va_c)


def _attn_b_kernel(q_ref, kl_ref, vl_ref, kc_ref, vc_ref, o_ref, k_scr, *, tq):
    n_lat = kl_ref.shape[1]
    k_scr[:n_lat] = kl_ref[0]
    k_scr[n_lat:] = kc_ref[0]
    group = GQA_HEADS // GQA_KV_HEADS

    for g in range(group):
        cols = slice(g * GQA_HEAD_DIM, (g + 1) * GQA_HEAD_DIM)

        def chunk(c, carry, cols=cols):
            rows = pl.ds(pl.multiple_of(c * tq, tq), tq)
            o = _softmax_pv(q_ref[0, rows, cols], k_scr[...], vl_ref[0], vc_ref[0], n_lat)
            o_ref[0, rows, cols] = o.astype(BF16)
            return carry

        lax.fori_loop(0, q_ref.shape[1] // tq, chunk, 0)


def _attn_b_call(qb, kb_l, vb_l, kb_c, vb_c):
    b, s, _ = qb.shape
    c = kb_c.shape[1]
    tq = min(TQ_ATTN, s)
    gw = (GQA_HEADS // GQA_KV_HEADS) * GQA_HEAD_DIM

    def kv(rows):
        return pl.BlockSpec((1, rows, GQA_HEAD_DIM), lambda bi, g: (bi, 0, g))

    qo = pl.BlockSpec((1, s, gw), lambda bi, g: (bi, 0, g))
    return pl.pallas_call(
        functools.partial(_attn_b_kernel, tq=tq),
        out_shape=jax.ShapeDtypeStruct((b, s, GQA_HEADS * GQA_HEAD_DIM), BF16),
        grid=(b, GQA_KV_HEADS),
        in_specs=[qo, kv(s), kv(s), kv(c), kv(c)],
        out_specs=qo,
        scratch_shapes=[pltpu.VMEM((s + c, GQA_HEAD_DIM), BF16)],
        compiler_params=_params(2),
        name="attn_b",
    )(qb, kb_l, vb_l, kb_c, vb_c)


def _merge_kernel(x_ref, mod_ref, n1g_ref, oa_ref, ob_ref, wga_ref, wgb_ref, wba_ref, wbb_ref,
                  wo_ref, n2g_ref, x1_ref, z2_ref, z_scr, m_scr):
    j = pl.program_id(2)
    nj = pl.num_programs(2)
    mod = mod_ref[0]

    @pl.when(j == 0)
    def _():
        z_scr[...] = (_rms(x_ref[0], n1g_ref[...]) * (1.0 + mod[1:2]) + mod[0:1]).astype(BF16)

    z = z_scr[...]
    ga = _sigmoid(jnp.dot(z, wga_ref[...], preferred_element_type=F32))
    gb = _sigmoid(jnp.dot(z, wgb_ref[...], preferred_element_type=F32))
    ba = jnp.dot(oa_ref[0], wba_ref[...], preferred_element_type=F32)
    bb = jnp.dot(ob_ref[0], wbb_ref[...], preferred_element_type=F32)
    m_scr[j] = (ga * ba + gb * bb).astype(BF16)

    @pl.when(j == nj - 1)
    def _():
        tc = m_scr.shape[2]
        a = jnp.dot(m_scr[0], wo_ref[:tc, :], preferred_element_type=F32)
        for k in range(1, m_scr.shape[0]):
            a = a + jnp.dot(m_scr[k], wo_ref[k * tc:(k + 1) * tc, :], preferred_element_type=F32)
        x1 = x_ref[0] + mod[2:3] * a
        x1_ref[0] = x1
        z2_ref[0] = (_rms(x1, n2g_ref[...]) * (1.0 + mod[4:5]) + mod[3:4]).astype(BF16)


def _merge_call(x, mod3, n1g, oa, ob, wga, wgb, wba, wbb, wo, n2g):
    b, s, d = x.shape
    tm = min(TM_MERGE, s)
    tc = TC_MERGE
    nj = d // tc
    ka, kb = wba.shape[0], wbb.shape[0]

    def tok(width):
        return pl.BlockSpec((1, tm, width), lambda bi, i, j: (bi, i, 0))

    def wcol(rows):
        return pl.BlockSpec((rows, tc), lambda bi, i, j: (0, j))

    return pl.pallas_call(
        _merge_kernel,
        out_shape=[jax.ShapeDtypeStruct((b, s, d), F32), jax.ShapeDtypeStruct((b, s, d), BF16)],
        grid=(b, s // tm, nj),
        in_specs=[tok(d), pl.BlockSpec((1, 6, d), lambda bi, i, j: (bi, 0, 0)), _resident((1, d)),
                  tok(ka), tok(kb), wcol(d), wcol(d), wcol(ka), wcol(kb),
                  _resident((d, d)), _resident((1, d))],
        out_specs=[tok(d), tok(d)],
        scratch_shapes=[pltpu.VMEM((tm, d), BF16), pltpu.VMEM((nj, tm, tc), BF16)],
        compiler_params=_params(3),
        name="merge",
    )(x, mod3, n1g, oa, ob, wga, wgb, wba, wbb, wo, n2g)


def _ffn_kernel(z_ref, zp_ref, zn_ref, x1_ref, mod_ref, wa_ref, wb_ref, cwa_ref, cwb_ref,
                cba_ref, cbb_ref, wd_ref, fg_ref, o_ref, zext, acc):
    i = pl.program_id(1)
    f = pl.program_id(2)
    tm = z_ref.shape[1]
    halo = zp_ref.shape[1]

    @pl.when(f == 0)
    def _():
        zext[:halo] = jnp.where(i > 0, zp_ref[0], jnp.zeros_like(zp_ref[0]))
        zext[halo:halo + tm] = z_ref[0]
        zext[halo + tm:] = jnp.where(i < pl.num_programs(1) - 1, zn_ref[0],
                                     jnp.zeros_like(zn_ref[0]))
        acc[...] = jnp.zeros_like(acc)

    ze = zext[...]
    n_ext = ze.shape[0]

    def conv(u, cw_ref, cb_ref):
        cw = cw_ref[...]
        prev = pltpu.roll(u, 1, 0)[halo:halo + tm]
        nxt = pltpu.roll(u, n_ext - 1, 0)[halo:halo + tm]
        return cb_ref[...] + cw[0:1] * prev + cw[1:2] * u[halo:halo + tm] + cw[2:3] * nxt

    ua = conv(jnp.dot(ze, wa_ref[...], preferred_element_type=F32), cwa_ref, cba_ref)
    ub = conv(jnp.dot(ze, wb_ref[...], preferred_element_type=F32), cwb_ref, cbb_ref)
    h = (ua * _sigmoid(ua) * ub).astype(BF16)
    acc[...] += jnp.dot(h, wd_ref[...], preferred_element_type=F32)

    @pl.when(f == pl.num_programs(2) - 1)
    def _():
        x2 = x1_ref[0] + mod_ref[0][5:6] * acc[...]
        o_ref[0] = _rms(x2, fg_ref[...])


def _ffn_call(z2, x1, mod3, w_up, conv_w, conv_b, w_down, fg):
    b, s, d = x1.shape
    dff = w_down.shape[0]
    tm = min(TM_FFN, s)
    tf = TF_FFN
    nf = dff // tf
    halo = BF16_SUBLANES
    hb = tm // halo
    last_hb = s // halo - 1

    def tok():
        return pl.BlockSpec((1, tm, d), lambda bi, i, f: (bi, i, 0))

    prev = pl.BlockSpec((1, halo, d), lambda bi, i, f: (bi, jnp.maximum(i * hb - 1, 0), 0))
    nxt = pl.BlockSpec((1, halo, d), lambda bi, i, f: (bi, jnp.minimum((i + 1) * hb, last_hb), 0))

    def up_cols(rows, second):
        off = nf if second else 0
        return pl.BlockSpec((rows, tf), lambda bi, i, f: (0, f + off))

    return pl.pallas_call(
        _ffn_kernel,
        out_shape=jax.ShapeDtypeStruct((b, s, d), F32),
        grid=(b, s // tm, nf),
        in_specs=[tok(), prev, nxt, tok(),
                  pl.BlockSpec((1, 6, d), lambda bi, i, f: (bi, 0, 0)),
                  up_cols(d, False), up_cols(d, True),
                  up_cols(CONV_W, False), up_cols(CONV_W, True),
                  up_cols(1, False), up_cols(1, True),
                  pl.BlockSpec((tf, d), lambda bi, i, f: (f, 0)),
                  _resident((1, d))],
        out_specs=tok(),
        scratch_shapes=[pltpu.VMEM((tm + 2 * halo, d), BF16), pltpu.VMEM((tm, d), F32)],
        compiler_params=_params(3),
        name="ffn",
    )(z2, z2, z2, x1, mod3, w_up, w_up, conv_w, conv_w, conv_b, conv_b, w_down, fg)


def _rope_table(n_rows, rot_dim):
    row = jnp.repeat(jnp.arange(n_rows, dtype=F32), GRID_W)
    col = jnp.tile(jnp.arange(GRID_W, dtype=F32), n_rows)
    half = rot_dim // 2
    inv_freq = ROPE_THETA ** (-jnp.arange(0, half, 2, dtype=F32) / half)
    t = row.shape[0]
    quarter = rot_dim // 4
    ang = jnp.stack([row[:, None] * inv_freq, col[:, None] * inv_freq], axis=1)
    cos = jnp.broadcast_to(jnp.cos(ang)[:, :, None, :], (t, 2, 2, quarter))
    sin = jnp.sin(ang)
    sin = jnp.stack([-sin, sin], axis=2)
    pad = ((0, 0), (0, LANES - rot_dim))
    return jnp.pad(cos.reshape(t, rot_dim), pad), jnp.pad(sin.reshape(t, rot_dim), pad)


def kernel(x, c, ctx, c_ctx, w_ada, b_ada, norm1_g, w_in, mla_q_norm_g, w_q_up, mla_kv_norm_g,
           w_kv_up, gqa_q_norm_g, gqa_k_norm_g, w_br_a, w_br_b, w_out, norm2_g, w_up, conv_w,
           conv_b, w_down, final_norm_g):
    bsz, seq, d = x.shape
    n_ctx = ctx.shape[1]
    depth = w_in.shape[0]
    assert depth == 1 and seq % GRID_W == 0 and bsz < ADA_ROWS

    kv_lora, rope_a = MLA_KV_LORA, MLA_ROPE
    kb_w = GQA_KV_HEADS * GQA_HEAD_DIM
    kv_cols = kv_lora + rope_a + 2 * kb_w
    q_cols = MLA_Q_LORA + GQA_HEADS * GQA_HEAD_DIM

    wi = w_in[0]
    c_kv = wi[:, :kv_lora]
    k_pe = wi[:, kv_lora:kv_lora + rope_a]
    k_b = wi[:, kv_lora + rope_a:kv_lora + rope_a + kb_w]
    v_b = wi[:, kv_lora + rope_a + kb_w:kv_cols]
    q_part = wi[:, kv_cols:kv_cols + q_cols]
    w_small = jnp.concatenate(
        [c_kv, k_b, v_b, k_pe, jnp.zeros((d, LANES - rope_a), wi.dtype), q_part], axis=1
    ).astype(BF16)
    gates = wi[:, kv_cols + q_cols:].astype(BF16)
    w_ga, w_gb = gates[:, :d], gates[:, d:]

    wkv = w_kv_up[0].reshape(kv_lora, MLA_HEADS, 2, MLA_NOPE).transpose(0, 2, 1, 3)
    wkv = wkv.reshape(kv_lora, 2 * MLA_HEADS * MLA_NOPE).astype(BF16)
    wq = w_q_up[0].reshape(MLA_Q_LORA, MLA_HEADS, MLA_NOPE + MLA_ROPE)
    wq = jnp.pad(wq, ((0, 0), (0, 0), (0, QA_HEAD_W - MLA_NOPE - MLA_ROPE)))
    wq = wq.reshape(MLA_Q_LORA, MLA_HEADS * QA_HEAD_W).astype(BF16)

    rope_tabs = (*_rope_table(seq // GRID_W, MLA_ROPE), *_rope_table(seq // GRID_W, GQA_HEAD_DIM))

    cond = jnp.zeros((ADA_ROWS, d), F32).at[:bsz].set(c).at[bsz].set(c_ctx)
    mod3 = _ada_call(cond, w_ada[0], b_ada).reshape(ADA_ROWS, 6, d)

    n1g = norm1_g
    gkv, gkb = mla_kv_norm_g, gqa_k_norm_g
    gq, gqb = mla_q_norm_g, gqa_q_norm_g

    kn_c, va_c, kpe_c, kb_c, vb_c = _kvq_call(
        ctx, mod3, lambda bi: bsz, n1g, w_small[:, :_KV_END], gkv, wkv, gkb, None, None,
        min(TM_KVQ, n_ctx))
    kn_l, va_l, kpe_l, kb_l, vb_l, qa, qb = _kvq_call(
        x, mod3, lambda bi: bi, n1g, w_small, gkv, wkv, gkb, (gq, wq, gqb), rope_tabs,
        min(TM_KVQ, seq))

    oa = _attn_a_call(qa, kn_l, kpe_l, va_l, kn_c, kpe_c, va_c)
    ob = _attn_b_call(qb, kb_l, vb_l, kb_c, vb_c)

    x1, z2 = _merge_call(x, mod3, n1g, oa, ob, w_ga, w_gb, w_br_a[0].astype(BF16),
                         w_br_b[0].astype(BF16), w_out[0].astype(BF16), norm2_g)
    return _ffn_call(z2, x1, mod3, w_up[0].astype(BF16), conv_w[0], conv_b, w_down[0].astype(BF16),
                     final_norm_g.reshape(1, d))
```

```python
import functools

import jax
import jax.numpy as jnp
from jax import lax
from jax.experimental import pallas as pl
from jax.experimental.pallas import tpu as pltpu

GRID_W = 64
ROPE_THETA = 10000.0
NORM_EPS = 1e-6
LOG2_E = 1.4426950408889634
MLA_HEADS = 8
MLA_Q_LORA = 768
MLA_KV_LORA = 512
MLA_NOPE = 128
MLA_ROPE = 64
MLA_V = 128
GQA_HEADS = 8
GQA_KV_HEADS = 2
GQA_HEAD_DIM = 128
CONV_W = 3
N_BRANCH = 2

LANES = 128
BF16_SUBLANES = 16
VMEM_LIMIT_BYTES = 56 * 1024 * 1024

TM_KVQ = 512
TM_MERGE = 512
TC_MERGE = 512
TM_FFN = 512
TF_FFN = 512
TQ_ATTN = 512
HEADS_PER_STEP_A = 4
KEY_BLOCK = 256
ONES_ROWS = BF16_SUBLANES
TN_ADA = 1536
ADA_ROWS = 16

F32 = jnp.float32
BF16 = jnp.bfloat16

_C_KV = 0
_K_B = _C_KV + MLA_KV_LORA
_V_B = _K_B + GQA_KV_HEADS * GQA_HEAD_DIM
_K_PE = _V_B + GQA_KV_HEADS * GQA_HEAD_DIM
_KV_END = _K_PE + LANES
_C_Q = _KV_END
_Q_B = _C_Q + MLA_Q_LORA
_Q_END = _Q_B + GQA_HEADS * GQA_HEAD_DIM
QA_HEAD_W = 2 * LANES


def _params(n_axes, flags=None):
    return pltpu.CompilerParams(
        dimension_semantics=("arbitrary",) * n_axes,
        vmem_limit_bytes=VMEM_LIMIT_BYTES,
        flags=flags,
    )


def _resident(shape):
    zeros = (0,) * len(shape)
    return pl.BlockSpec(shape, lambda *_: zeros, pipeline_mode=pl.Buffered(1))


def _rms(x, g):
    return x * lax.rsqrt(jnp.mean(x * x, axis=-1, keepdims=True) + NORM_EPS) * g


def _rope(t, cos, sin_signed, quarter):
    lane = lax.broadcasted_iota(jnp.int32, t.shape, 1)
    first = (lane % (2 * quarter)) < quarter
    partner = jnp.where(first, pltpu.roll(t, LANES - quarter, 1), pltpu.roll(t, quarter, 1))
    return t * cos + partner * sin_signed


def _sigmoid(x):
    return 1.0 / (1.0 + jnp.exp(-x))


def _ada_kernel(c_ref, w_ref, b_ref, o_ref):
    c = c_ref[...]
    s = (c * _sigmoid(c)).astype(BF16)
    o_ref[...] = jnp.dot(s, w_ref[...].astype(BF16), preferred_element_type=F32) + b_ref[...]


def _ada_call(cond, w_ada, b_ada):
    d, n = w_ada.shape
    return pl.pallas_call(
        _ada_kernel,
        out_shape=jax.ShapeDtypeStruct((ADA_ROWS, n), F32),
        grid=(n // TN_ADA,),
        in_specs=[
            pl.BlockSpec((ADA_ROWS, d), lambda j: (0, 0)),
            pl.BlockSpec((d, TN_ADA), lambda j: (0, j)),
            pl.BlockSpec((1, TN_ADA), lambda j: (0, j)),
        ],
        out_specs=pl.BlockSpec((ADA_ROWS, TN_ADA), lambda j: (0, j)),
        compiler_params=_params(1),
        name="ada",
    )(cond, w_ada, b_ada)


def _kvq_kernel(*refs, with_q, with_rope):
    it = iter(refs)
    x_ref, mod_ref, n1g_ref, w_ref, gkv_ref, wkv_ref, gkb_ref = (next(it) for _ in range(7))
    if with_q:
        gq_ref, wq_ref, gqb_ref = (next(it) for _ in range(3))
    if with_rope:
        ca_ref, sa_ref, cb_ref, sb_ref = (next(it) for _ in range(4))
    kn_ref, va_ref, kpe_ref, kb_ref, vb_ref = (next(it) for _ in range(5))
    if with_q:
        qa_ref, qb_ref = (next(it) for _ in range(2))

    mod = mod_ref[0]
    z = (_rms(x_ref[0], n1g_ref[...]) * (1.0 + mod[1:2]) + mod[0:1]).astype(BF16)
    proj = jnp.dot(z, w_ref[...], preferred_element_type=F32)

    if with_rope:
        ca, sa, cb, sb = ca_ref[...], sa_ref[...], cb_ref[...], sb_ref[...]

    ckv = _rms(proj[:, _C_KV:_K_B], gkv_ref[...]).astype(BF16)
    kv_up = jnp.dot(ckv, wkv_ref[...], preferred_element_type=F32)
    hn = MLA_HEADS * MLA_NOPE
    kn_ref[0] = kv_up[:, :hn].astype(BF16)
    va_ref[0] = kv_up[:, hn:].astype(BF16)

    kpe = proj[:, _K_PE:_KV_END]
    if with_rope:
        kpe = _rope(kpe, ca, sa, MLA_ROPE // 4)
    kpe_ref[0] = kpe.astype(BF16)

    for h in range(GQA_KV_HEADS):
        lo = _K_B + h * GQA_HEAD_DIM
        t = _rms(proj[:, lo:lo + GQA_HEAD_DIM], gkb_ref[...])
        if with_rope:
            t = _rope(t, cb, sb, GQA_HEAD_DIM // 4)
        kb_ref[0, :, h * GQA_HEAD_DIM:(h + 1) * GQA_HEAD_DIM] = t.astype(BF16)
    vb_ref[0] = proj[:, _V_B:_K_PE].astype(BF16)

    if with_q:
        scale_a = LOG2_E * float(MLA_NOPE + MLA_ROPE) ** -0.5
        scale_b = LOG2_E * float(GQA_HEAD_DIM) ** -0.5
        cq = _rms(proj[:, _C_Q:_Q_B], gq_ref[...]).astype(BF16)
        q_up = jnp.dot(cq, wq_ref[...], preferred_element_type=F32)
        for h in range(MLA_HEADS):
            lo = h * QA_HEAD_W
            qa_ref[0, :, lo:lo + LANES] = (q_up[:, lo:lo + LANES] * scale_a).astype(BF16)
            pe = q_up[:, lo + LANES:lo + QA_HEAD_W]
            if with_rope:
                pe = _rope(pe, ca, sa, MLA_ROPE // 4)
            qa_ref[0, :, lo + LANES:lo + QA_HEAD_W] = (pe * scale_a).astype(BF16)
        for h in range(GQA_HEADS):
            lo = _Q_B + h * GQA_HEAD_DIM
            t = _rms(proj[:, lo:lo + GQA_HEAD_DIM], gqb_ref[...])
            if with_rope:
                t = _rope(t, cb, sb, GQA_HEAD_DIM // 4)
            qb_ref[0, :, h * GQA_HEAD_DIM:(h + 1) * GQA_HEAD_DIM] = (t * scale_b).astype(BF16)


def _kvq_call(x, mod3, mod_row, n1g, w_small, gkv, wkv, gkb, q_parts, rope_tabs, tm):
    b, t, d = x.shape
    with_q = q_parts is not None
    with_rope = rope_tabs is not None
    nt = t // tm
    cols = w_small.shape[1]

    in_specs = [
        pl.BlockSpec((1, tm, d), lambda bi, i: (bi, i, 0)),
        pl.BlockSpec((1, 6, d), lambda bi, i: (mod_row(bi), 0, 0)),
        _resident((1, d)),
        _resident((d, cols)),
        _resident(gkv.shape),
        _resident(wkv.shape),
        _resident(gkb.shape),
    ]
    args = [x, mod3, n1g, w_small, gkv, wkv, gkb]
    if with_q:
        gq, wq, gqb = q_parts
        in_specs += [_resident(gq.shape), _resident(wq.shape), _resident(gqb.shape)]
        args += [gq, wq, gqb]
    if with_rope:
        in_specs += [pl.BlockSpec((tm, LANES), lambda bi, i: (i, 0))] * 4
        args += list(rope_tabs)

    def tok(width):
        return pl.BlockSpec((1, tm, width), lambda bi, i: (bi, i, 0))

    widths = [MLA_HEADS * MLA_NOPE, MLA_HEADS * MLA_V, LANES,
              GQA_KV_HEADS * GQA_HEAD_DIM, GQA_KV_HEADS * GQA_HEAD_DIM]
    if with_q:
        widths += [MLA_HEADS * QA_HEAD_W, GQA_HEADS * GQA_HEAD_DIM]
    return pl.pallas_call(
        functools.partial(_kvq_kernel, with_q=with_q, with_rope=with_rope),
        out_shape=[jax.ShapeDtypeStruct((b, t, w), BF16) for w in widths],
        grid=(b, nt),
        in_specs=in_specs,
        out_specs=[tok(w) for w in widths],
        compiler_params=_params(2),
        name="kvq_lat" if with_q else "kvq_ctx",
    )(*args)


def _rows(c, tq):
    if isinstance(c, int):
        return pl.ds(c * tq, tq)
    return pl.ds(pl.multiple_of(c * tq, tq), tq)


def _scores_t(k, q):
    return lax.dot_general(k, q, (((1,), (1,)), ((), ())), preferred_element_type=F32)


def _attn_pipeline(n_heads, n_chunks, load_q, k_of, vt_of, store_o, s_bufs, p_bufs, key_block):
    items = [(h, c) for h in range(n_heads) for c in range(n_chunks)]
    n_kb = s_bufs[0].shape[0] // key_block
    col_max = {}
    for t in range(-2, len(items)):
        qk_i, ex_i, pv_i = t + 2, t + 1, t
        do_qk, do_ex, do_pv = qk_i < len(items), 0 <= ex_i < len(items), pv_i >= 0
        if do_qk:
            k_ref, q = k_of(items[qk_i][0]), load_q(*items[qk_i])
        if do_pv:
            vt_ref = vt_of(items[pv_i][0])
        m_new = acc = None
        for kb in range(n_kb):
            rows = slice(kb * key_block, (kb + 1) * key_block)
            if do_qk:
                s_blk = _scores_t(k_ref[rows, :], q)
                s_bufs[qk_i % 2][rows, :] = s_blk
                bm = jnp.max(s_blk, axis=0, keepdims=True)
                m_new = bm if m_new is None else jnp.maximum(m_new, bm)
            if do_ex:
                s_blk = s_bufs[ex_i % 2][rows, :]
                p_bufs[ex_i % 2][rows, :] = jnp.exp2(s_blk - col_max[ex_i]).astype(BF16)
            if do_pv:
                d = jnp.dot(vt_ref[kb], p_bufs[pv_i % 2][rows, :], preferred_element_type=F32)
                acc = d if acc is None else acc + d
        if do_qk:
            col_max[qk_i] = m_new
        if do_pv:
            dv = acc.shape[0] - ONES_ROWS
            store_o(*items[pv_i], (acc[:dv] * (1.0 / acc[dv:dv + 1])).T)


def _fill_vt(vt_ref, v_lat, v_ctx):
    kb_w = vt_ref.shape[2]
    dv = vt_ref.shape[1] - ONES_ROWS
    blocks = [v_lat[i:i + kb_w] for i in range(0, v_lat.shape[0], kb_w)]
    blocks += [v_ctx[i:i + kb_w] for i in range(0, v_ctx.shape[0], kb_w)]
    for kb, blk in enumerate(blocks):
        vt_ref[kb, :dv, :] = blk.astype(F32).T.astype(BF16)
        vt_ref[kb, dv:, :] = jnp.ones((ONES_ROWS, kb_w), BF16)


def _attn_a_kernel(q_ref, knl_ref, kpl_ref, val_ref, knc_ref, kpc_ref, vac_ref, o_ref,
                   k_scr, vt_scr, s0, s1, p0, p1, *, tq, heads):
    n_lat = knl_ref.shape[1]
    for h in range(heads):
        cols = slice(h * LANES, (h + 1) * LANES)
        k_scr[h, :n_lat, :LANES] = knl_ref[0, :, cols]
        k_scr[h, :n_lat, LANES:] = kpl_ref[0]
        k_scr[h, n_lat:, :LANES] = knc_ref[0, :, cols]
        k_scr[h, n_lat:, LANES:] = kpc_ref[0]
        _fill_vt(vt_scr.at[h], val_ref[0, :, cols], vac_ref[0, :, cols])

    def load_q(h, c):
        return q_ref[0, _rows(c, tq), h * QA_HEAD_W:(h + 1) * QA_HEAD_W]

    def store_o(h, c, o):
        o_ref[0, _rows(c, tq), h * MLA_V:(h + 1) * MLA_V] = o.astype(BF16)

    _attn_pipeline(heads, q_ref.shape[1] // tq, load_q, lambda h: k_scr.at[h],
                   lambda h: vt_scr.at[h], store_o, (s0, s1), (p0, p1), KEY_BLOCK)


def _attn_a_call(qa, kn_l, kpe_l, va_l, kn_c, kpe_c, va_c):
    b, s, _ = qa.shape
    c = kn_c.shape[1]
    tq = min(TQ_ATTN, s)
    hs = HEADS_PER_STEP_A

    def heads(rows, width):
        return pl.BlockSpec((1, rows, hs * width), lambda bi, h: (bi, 0, h))

    def shared(rows):
        return pl.BlockSpec((1, rows, LANES), lambda bi, h: (bi, 0, 0))

    return pl.pallas_call(
        functools.partial(_attn_a_kernel, tq=tq, heads=hs),
        out_shape=jax.ShapeDtypeStruct((b, s, MLA_HEADS * MLA_V), BF16),
        grid=(b, MLA_HEADS // hs),
        in_specs=[heads(s, QA_HEAD_W), heads(s, MLA_NOPE), shared(s), heads(s, MLA_V),
                  heads(c, MLA_NOPE), shared(c), heads(c, MLA_V)],
        out_specs=heads(s, MLA_V),
        scratch_shapes=[pltpu.VMEM((hs, s + c, QA_HEAD_W), BF16),
                        pltpu.VMEM((hs, (s + c) // KEY_BLOCK, MLA_V + ONES_ROWS, KEY_BLOCK), BF16),
                        pltpu.VMEM((s + c, tq), F32), pltpu.VMEM((s + c, tq), F32),
                        pltpu.VMEM((s + c, tq), BF16), pltpu.VMEM((s + c, tq), BF16)],
        compiler_params=_params(2),
        name="attn_a",
    )(qa, kn_l, kpe_l, va_l, kn_c, kpe_c, va_c)


def _attn_b_kernel(q_ref, kl_ref, vl_ref, kc_ref, vc_ref, o_ref, k_scr, vt_scr, s0, s1, p0, p1,
                   *, tq):
    n_lat = kl_ref.shape[1]
    k_scr[:n_lat] = kl_ref[0]
    k_scr[n_lat:] = kc_ref[0]
    hd = GQA_HEAD_DIM
    _fill_vt(vt_scr, vl_ref[0], vc_ref[0])

    def load_q(h, c):
        return q_ref[0, _rows(c, tq), h * hd:(h + 1) * hd]

    def store_o(h, c, o):
        o_ref[0, _rows(c, tq), h * hd:(h + 1) * hd] = o.astype(BF16)

    _attn_pipeline(GQA_HEADS // GQA_KV_HEADS, q_ref.shape[1] // tq, load_q, lambda h: k_scr,
                   lambda h: vt_scr, store_o, (s0, s1), (p0, p1), KEY_BLOCK)


def _attn_b_call(qb, kb_l, vb_l, kb_c, vb_c):
    b, s, _ = qb.shape
    c = kb_c.shape[1]
    tq = min(TQ_ATTN, s)
    gw = (GQA_HEADS // GQA_KV_HEADS) * GQA_HEAD_DIM

    def kv(rows):
        return pl.BlockSpec((1, rows, GQA_HEAD_DIM), lambda bi, g: (bi, 0, g))

    qo = pl.BlockSpec((1, s, gw), lambda bi, g: (bi, 0, g))
    return pl.pallas_call(
        functools.partial(_attn_b_kernel, tq=tq),
        out_shape=jax.ShapeDtypeStruct((b, s, GQA_HEADS * GQA_HEAD_DIM), BF16),
        grid=(b, GQA_KV_HEADS),
        in_specs=[qo, kv(s), kv(s), kv(c), kv(c)],
        out_specs=qo,
        scratch_shapes=[pltpu.VMEM((s + c, GQA_HEAD_DIM), BF16),
                        pltpu.VMEM(((s + c) // KEY_BLOCK, GQA_HEAD_DIM + ONES_ROWS, KEY_BLOCK), BF16),
                        pltpu.VMEM((s + c, tq), F32), pltpu.VMEM((s + c, tq), F32),
                        pltpu.VMEM((s + c, tq), BF16), pltpu.VMEM((s + c, tq), BF16)],
        compiler_params=_params(2),
        name="attn_b",
    )(qb, kb_l, vb_l, kb_c, vb_c)


def _merge_kernel(x_ref, mod_ref, n1g_ref, oa_ref, ob_ref, wga_ref, wgb_ref, wba_ref, wbb_ref,
                  wo_ref, n2g_ref, x1_ref, z2_ref, z_scr, m_scr):
    j = pl.program_id(2)
    nj = pl.num_programs(2)
    mod = mod_ref[0]

    @pl.when(j == 0)
    def _():
        z_scr[...] = (_rms(x_ref[0], n1g_ref[...]) * (1.0 + mod[1:2]) + mod[0:1]).astype(BF16)

    z = z_scr[...]
    ga = _sigmoid(jnp.dot(z, wga_ref[...], preferred_element_type=F32))
    gb = _sigmoid(jnp.dot(z, wgb_ref[...], preferred_element_type=F32))
    ba = jnp.dot(oa_ref[0], wba_ref[...], preferred_element_type=F32)
    bb = jnp.dot(ob_ref[0], wbb_ref[...], preferred_element_type=F32)
    m_scr[j] = (ga * ba + gb * bb).astype(BF16)

    @pl.when(j == nj - 1)
    def _():
        tc = m_scr.shape[2]
        a = jnp.dot(m_scr[0], wo_ref[:tc, :], preferred_element_type=F32)
        for k in range(1, m_scr.shape[0]):
            a = a + jnp.dot(m_scr[k], wo_ref[k * tc:(k + 1) * tc, :], preferred_element_type=F32)
        x1 = x_ref[0] + mod[2:3] * a
        x1_ref[0] = x1
        z2_ref[0] = (_rms(x1, n2g_ref[...]) * (1.0 + mod[4:5]) + mod[3:4]).astype(BF16)


def _merge_call(x, mod3, n1g, oa, ob, wga, wgb, wba, wbb, wo, n2g):
    b, s, d = x.shape
    tm = min(TM_MERGE, s)
    tc = TC_MERGE
    nj = d // tc
    ka, kb = wba.shape[0], wbb.shape[0]

    def tok(width):
        return pl.BlockSpec((1, tm, width), lambda bi, i, j: (bi, i, 0))

    def wcol(rows):
        return pl.BlockSpec((rows, tc), lambda bi, i, j: (0, j))

    return pl.pallas_call(
        _merge_kernel,
        out_shape=[jax.ShapeDtypeStruct((b, s, d), F32), jax.ShapeDtypeStruct((b, s, d), BF16)],
        grid=(b, s // tm, nj),
        in_specs=[tok(d), pl.BlockSpec((1, 6, d), lambda bi, i, j: (bi, 0, 0)), _resident((1, d)),
                  tok(ka), tok(kb), wcol(d), wcol(d), wcol(ka), wcol(kb),
                  _resident((d, d)), _resident((1, d))],
        out_specs=[tok(d), tok(d)],
        scratch_shapes=[pltpu.VMEM((tm, d), BF16), pltpu.VMEM((nj, tm, tc), BF16)],
        compiler_params=_params(3),
        name="merge",
    )(x, mod3, n1g, oa, ob, wga, wgb, wba, wbb, wo, n2g)


def _ffn_kernel(z_ref, zp_ref, zn_ref, x1_ref, mod_ref, wa_ref, wb_ref, cwa_ref, cwb_ref,
                cba_ref, cbb_ref, wd_ref, fg_ref, o_ref, zext, acc):
    i = pl.program_id(1)
    f = pl.program_id(2)
    tm = z_ref.shape[1]
    halo = zp_ref.shape[1]

    @pl.when(f == 0)
    def _():
        zext[:halo] = jnp.where(i > 0, zp_ref[0], jnp.zeros_like(zp_ref[0]))
        zext[halo:halo + tm] = z_ref[0]
        zext[halo + tm:] = jnp.where(i < pl.num_programs(1) - 1, zn_ref[0],
                                     jnp.zeros_like(zn_ref[0]))
        acc[...] = jnp.zeros_like(acc)

    ze = zext[...]
    n_ext = ze.shape[0]

    def conv(u, cw_ref, cb_ref):
        cw = cw_ref[...]
        prev = pltpu.roll(u, 1, 0)[halo:halo + tm]
        nxt = pltpu.roll(u, n_ext - 1, 0)[halo:halo + tm]
        return cb_ref[...] + cw[0:1] * prev + cw[1:2] * u[halo:halo + tm] + cw[2:3] * nxt

    ua = conv(jnp.dot(ze, wa_ref[...], preferred_element_type=F32), cwa_ref, cba_ref)
    ub = conv(jnp.dot(ze, wb_ref[...], preferred_element_type=F32), cwb_ref, cbb_ref)
    h = (ua * _sigmoid(ua) * ub).astype(BF16)
    acc[...] += jnp.dot(h, wd_ref[...], preferred_element_type=F32)

    @pl.when(f == pl.num_programs(2) - 1)
    def _():
        x2 = x1_ref[0] + mod_ref[0][5:6] * acc[...]
        o_ref[0] = _rms(x2, fg_ref[...])


def _ffn_call(z2, x1, mod3, w_up, conv_w, conv_b, w_down, fg):
    b, s, d = x1.shape
    dff = w_down.shape[0]
    tm = min(TM_FFN, s)
    tf = TF_FFN
    nf = dff // tf
    halo = BF16_SUBLANES
    hb = tm // halo
    last_hb = s // halo - 1

    def tok():
        return pl.BlockSpec((1, tm, d), lambda bi, i, f: (bi, i, 0))

    prev = pl.BlockSpec((1, halo, d), lambda bi, i, f: (bi, jnp.maximum(i * hb - 1, 0), 0))
    nxt = pl.BlockSpec((1, halo, d), lambda bi, i, f: (bi, jnp.minimum((i + 1) * hb, last_hb), 0))

    def up_cols(rows, second):
        off = nf if second else 0
        return pl.BlockSpec((rows, tf), lambda bi, i, f: (0, f + off))

    return pl.pallas_call(
        _ffn_kernel,
        out_shape=jax.ShapeDtypeStruct((b, s, d), F32),
        grid=(b, s // tm, nf),
        in_specs=[tok(), prev, nxt, tok(),
                  pl.BlockSpec((1, 6, d), lambda bi, i, f: (bi, 0, 0)),
                  up_cols(d, False), up_cols(d, True),
                  up_cols(CONV_W, False), up_cols(CONV_W, True),
                  up_cols(1, False), up_cols(1, True),
                  pl.BlockSpec((tf, d), lambda bi, i, f: (f, 0)),
                  _resident((1, d))],
        out_specs=tok(),
        scratch_shapes=[pltpu.VMEM((tm + 2 * halo, d), BF16), pltpu.VMEM((tm, d), F32)],
        compiler_params=_params(3),
        name="ffn",
    )(z2, z2, z2, x1, mod3, w_up, w_up, conv_w, conv_w, conv_b, conv_b, w_down, fg)


def _rope_table(n_rows, rot_dim):
    row = jnp.repeat(jnp.arange(n_rows, dtype=F32), GRID_W)
    col = jnp.tile(jnp.arange(GRID_W, dtype=F32), n_rows)
    half = rot_dim // 2
    inv_freq = ROPE_THETA ** (-jnp.arange(0, half, 2, dtype=F32) / half)
    t = row.shape[0]
    quarter = rot_dim // 4
    ang = jnp.stack([row[:, None] * inv_freq, col[:, None] * inv_freq], axis=1)
    cos = jnp.broadcast_to(jnp.cos(ang)[:, :, None, :], (t, 2, 2, quarter))
    sin = jnp.sin(ang)
    sin = jnp.stack([-sin, sin], axis=2)
    pad = ((0, 0), (0, LANES - rot_dim))
    return jnp.pad(cos.reshape(t, rot_dim), pad), jnp.pad(sin.reshape(t, rot_dim), pad)


def kernel(x, c, ctx, c_ctx, w_ada, b_ada, norm1_g, w_in, mla_q_norm_g, w_q_up, mla_kv_norm_g,
           w_kv_up, gqa_q_norm_g, gqa_k_norm_g, w_br_a, w_br_b, w_out, norm2_g, w_up, conv_w,
           conv_b, w_down, final_norm_g):
    bsz, seq, d = x.shape
    n_ctx = ctx.shape[1]
    depth = w_in.shape[0]
    assert depth == 1 and seq % GRID_W == 0 and bsz < ADA_ROWS

    kv_lora, rope_a = MLA_KV_LORA, MLA_ROPE
    kb_w = GQA_KV_HEADS * GQA_HEAD_DIM
    kv_cols = kv_lora + rope_a + 2 * kb_w
    q_cols = MLA_Q_LORA + GQA_HEADS * GQA_HEAD_DIM

    wi = w_in[0]
    c_kv = wi[:, :kv_lora]
    k_pe = wi[:, kv_lora:kv_lora + rope_a]
    k_b = wi[:, kv_lora + rope_a:kv_lora + rope_a + kb_w]
    v_b = wi[:, kv_lora + rope_a + kb_w:kv_cols]
    q_part = wi[:, kv_cols:kv_cols + q_cols]
    w_small = jnp.concatenate(
        [c_kv, k_b, v_b, k_pe, jnp.zeros((d, LANES - rope_a), wi.dtype), q_part], axis=1
    ).astype(BF16)
    gates = wi[:, kv_cols + q_cols:].astype(BF16)
    w_ga, w_gb = gates[:, :d], gates[:, d:]

    wkv = w_kv_up[0].reshape(kv_lora, MLA_HEADS, 2, MLA_NOPE).transpose(0, 2, 1, 3)
    wkv = wkv.reshape(kv_lora, 2 * MLA_HEADS * MLA_NOPE).astype(BF16)
    wq = w_q_up[0].reshape(MLA_Q_LORA, MLA_HEADS, MLA_NOPE + MLA_ROPE)
    wq = jnp.pad(wq, ((0, 0), (0, 0), (0, QA_HEAD_W - MLA_NOPE - MLA_ROPE)))
    wq = wq.reshape(MLA_Q_LORA, MLA_HEADS * QA_HEAD_W).astype(BF16)

    rope_tabs = (*_rope_table(seq // GRID_W, MLA_ROPE), *_rope_table(seq // GRID_W, GQA_HEAD_DIM))

    cond = jnp.zeros((ADA_ROWS, d), F32).at[:bsz].set(c).at[bsz].set(c_ctx)
    mod3 = _ada_call(cond, w_ada[0], b_ada).reshape(ADA_ROWS, 6, d)

    n1g = norm1_g
    gkv, gkb = mla_kv_norm_g, gqa_k_norm_g
    gq, gqb = mla_q_norm_g, gqa_q_norm_g

    kn_c, va_c, kpe_c, kb_c, vb_c = _kvq_call(
        ctx, mod3, lambda bi: bsz, n1g, w_small[:, :_KV_END], gkv, wkv, gkb, None, None,
        min(TM_KVQ, n_ctx))
    kn_l, va_l, kpe_l, kb_l, vb_l, qa, qb = _kvq_call(
        x, mod3, lambda bi: bi, n1g, w_small, gkv, wkv, gkb, (gq, wq, gqb), rope_tabs,
        min(TM_KVQ, seq))

    oa = _attn_a_call(qa, kn_l, kpe_l, va_l, kn_c, kpe_c, va_c)
    ob = _attn_b_call(qb, kb_l, vb_l, kb_c, vb_c)

    x1, z2 = _merge_call(x, mod3, n1g, oa, ob, w_ga, w_gb, w_br_a[0].astype(BF16),
                         w_br_b[0].astype(BF16), w_out[0].astype(BF16), norm2_g)
    return _ffn_call(z2, x1, mod3, w_up[0].astype(BF16), conv_w[0], conv_b, w_down[0].astype(BF16),
                     final_norm_g.reshape(1, d))
```

```python
import functools

import jax
import jax.numpy as jnp
from jax import lax
from jax.experimental import pallas as pl
from jax.experimental.pallas import tpu as pltpu

GRID_W = 64
ROPE_THETA = 10000.0
NORM_EPS = 1e-6
LOG2_E = 1.4426950408889634
MLA_HEADS = 8
MLA_Q_LORA = 768
MLA_KV_LORA = 512
MLA_NOPE = 128
MLA_ROPE = 64
MLA_V = 128
GQA_HEADS = 8
GQA_KV_HEADS = 2
GQA_HEAD_DIM = 128
CONV_W = 3
N_BRANCH = 2

LANES = 128
BF16_SUBLANES = 16
VMEM_LIMIT_BYTES = 56 * 1024 * 1024

TM_KVQ = 512
TM_MERGE = 512
TC_MERGE = 512
TM_FFN = 512
TF_FFN = 512
FFN_PIECES = 8
TQ_ATTN = 512
HEADS_PER_STEP_A = 4
KEY_BLOCK = 256
ONES_ROWS = BF16_SUBLANES
TN_ADA = 1536
ADA_ROWS = 16

F32 = jnp.float32
BF16 = jnp.bfloat16

_C_KV = 0
_K_B = _C_KV + MLA_KV_LORA
_V_B = _K_B + GQA_KV_HEADS * GQA_HEAD_DIM
_K_PE = _V_B + GQA_KV_HEADS * GQA_HEAD_DIM
_KV_END = _K_PE + LANES
_C_Q = _KV_END
_Q_B = _C_Q + MLA_Q_LORA
_Q_END = _Q_B + GQA_HEADS * GQA_HEAD_DIM
QA_HEAD_W = 2 * LANES


def _params(n_axes, flags=None):
    return pltpu.CompilerParams(
        dimension_semantics=("arbitrary",) * n_axes,
        vmem_limit_bytes=VMEM_LIMIT_BYTES,
        flags=flags,
    )


def _resident(shape):
    zeros = (0,) * len(shape)
    return pl.BlockSpec(shape, lambda *_: zeros, pipeline_mode=pl.Buffered(1))


def _rms(x, g):
    return x * lax.rsqrt(jnp.mean(x * x, axis=-1, keepdims=True) + NORM_EPS) * g


def _rope(t, cos, sin_signed, quarter):
    lane = lax.broadcasted_iota(jnp.int32, t.shape, 1)
    first = (lane % (2 * quarter)) < quarter
    partner = jnp.where(first, pltpu.roll(t, LANES - quarter, 1), pltpu.roll(t, quarter, 1))
    return t * cos + partner * sin_signed


def _sigmoid(x):
    return 1.0 / (1.0 + jnp.exp(-x))


def _ada_kernel(c_ref, w_ref, b_ref, o_ref):
    c = c_ref[...]
    s = (c * _sigmoid(c)).astype(BF16)
    o_ref[...] = jnp.dot(s, w_ref[...].astype(BF16), preferred_element_type=F32) + b_ref[...]


def _ada_call(cond, w_ada, b_ada):
    d, n = w_ada.shape
    return pl.pallas_call(
        _ada_kernel,
        out_shape=jax.ShapeDtypeStruct((ADA_ROWS, n), F32),
        grid=(n // TN_ADA,),
        in_specs=[
            pl.BlockSpec((ADA_ROWS, d), lambda j: (0, 0)),
            pl.BlockSpec((d, TN_ADA), lambda j: (0, j)),
            pl.BlockSpec((1, TN_ADA), lambda j: (0, j)),
        ],
        out_specs=pl.BlockSpec((ADA_ROWS, TN_ADA), lambda j: (0, j)),
        compiler_params=_params(1),
        name="ada",
    )(cond, w_ada, b_ada)


def _kvq_kernel(*refs, with_q, with_rope):
    it = iter(refs)
    x_ref, mod_ref, n1g_ref, w_ref, gkv_ref, wkv_ref, gkb_ref = (next(it) for _ in range(7))
    if with_q:
        gq_ref, wq_ref, gqb_ref = (next(it) for _ in range(3))
    if with_rope:
        ca_ref, sa_ref, cb_ref, sb_ref = (next(it) for _ in range(4))
    kn_ref, va_ref, kpe_ref, kb_ref, vb_ref = (next(it) for _ in range(5))
    if with_q:
        qa_ref, qb_ref = (next(it) for _ in range(2))

    mod = mod_ref[0]
    z = (_rms(x_ref[0], n1g_ref[...]) * (1.0 + mod[1:2]) + mod[0:1]).astype(BF16)
    proj = jnp.dot(z, w_ref[...], preferred_element_type=F32)

    if with_rope:
        ca, sa, cb, sb = ca_ref[...], sa_ref[...], cb_ref[...], sb_ref[...]

    ckv = _rms(proj[:, _C_KV:_K_B], gkv_ref[...]).astype(BF16)
    kv_up = jnp.dot(ckv, wkv_ref[...], preferred_element_type=F32)
    hn = MLA_HEADS * MLA_NOPE
    kn_ref[0] = kv_up[:, :hn].astype(BF16)
    va_ref[0] = kv_up[:, hn:].astype(BF16)

    kpe = proj[:, _K_PE:_KV_END]
    if with_rope:
        kpe = _rope(kpe, ca, sa, MLA_ROPE // 4)
    kpe_ref[0] = kpe.astype(BF16)

    for h in range(GQA_KV_HEADS):
        lo = _K_B + h * GQA_HEAD_DIM
        t = _rms(proj[:, lo:lo + GQA_HEAD_DIM], gkb_ref[...])
        if with_rope:
            t = _rope(t, cb, sb, GQA_HEAD_DIM // 4)
        kb_ref[0, :, h * GQA_HEAD_DIM:(h + 1) * GQA_HEAD_DIM] = t.astype(BF16)
    vb_ref[0] = proj[:, _V_B:_K_PE].astype(BF16)

    if with_q:
        scale_a = LOG2_E * float(MLA_NOPE + MLA_ROPE) ** -0.5
        scale_b = LOG2_E * float(GQA_HEAD_DIM) ** -0.5
        cq = _rms(proj[:, _C_Q:_Q_B], gq_ref[...]).astype(BF16)
        q_up = jnp.dot(cq, wq_ref[...], preferred_element_type=F32)
        for h in range(MLA_HEADS):
            lo = h * QA_HEAD_W
            qa_ref[0, :, lo:lo + LANES] = (q_up[:, lo:lo + LANES] * scale_a).astype(BF16)
            pe = q_up[:, lo + LANES:lo + QA_HEAD_W]
            if with_rope:
                pe = _rope(pe, ca, sa, MLA_ROPE // 4)
            qa_ref[0, :, lo + LANES:lo + QA_HEAD_W] = (pe * scale_a).astype(BF16)
        for h in range(GQA_HEADS):
            lo = _Q_B + h * GQA_HEAD_DIM
            t = _rms(proj[:, lo:lo + GQA_HEAD_DIM], gqb_ref[...])
            if with_rope:
                t = _rope(t, cb, sb, GQA_HEAD_DIM // 4)
            qb_ref[0, :, h * GQA_HEAD_DIM:(h + 1) * GQA_HEAD_DIM] = (t * scale_b).astype(BF16)


def _kvq_call(x, mod3, mod_row, n1g, w_small, gkv, wkv, gkb, q_parts, rope_tabs, tm):
    b, t, d = x.shape
    with_q = q_parts is not None
    with_rope = rope_tabs is not None
    nt = t // tm
    cols = w_small.shape[1]

    in_specs = [
        pl.BlockSpec((1, tm, d), lambda bi, i: (bi, i, 0)),
        pl.BlockSpec((1, 6, d), lambda bi, i: (mod_row(bi), 0, 0)),
        _resident((1, d)),
        _resident((d, cols)),
        _resident(gkv.shape),
        _resident(wkv.shape),
        _resident(gkb.shape),
    ]
    args = [x, mod3, n1g, w_small, gkv, wkv, gkb]
    if with_q:
        gq, wq, gqb = q_parts
        in_specs += [_resident(gq.shape), _resident(wq.shape), _resident(gqb.shape)]
        args += [gq, wq, gqb]
    if with_rope:
        in_specs += [pl.BlockSpec((tm, LANES), lambda bi, i: (i, 0))] * 4
        args += list(rope_tabs)

    def tok(width):
        return pl.BlockSpec((1, tm, width), lambda bi, i: (bi, i, 0))

    widths = [MLA_HEADS * MLA_NOPE, MLA_HEADS * MLA_V, LANES,
              GQA_KV_HEADS * GQA_HEAD_DIM, GQA_KV_HEADS * GQA_HEAD_DIM]
    if with_q:
        widths += [MLA_HEADS * QA_HEAD_W, GQA_HEADS * GQA_HEAD_DIM]
    return pl.pallas_call(
        functools.partial(_kvq_kernel, with_q=with_q, with_rope=with_rope),
        out_shape=[jax.ShapeDtypeStruct((b, t, w), BF16) for w in widths],
        grid=(b, nt),
        in_specs=in_specs,
        out_specs=[tok(w) for w in widths],
        compiler_params=_params(2),
        name="kvq_lat" if with_q else "kvq_ctx",
    )(*args)


def _rows(c, tq):
    if isinstance(c, int):
        return pl.ds(c * tq, tq)
    return pl.ds(pl.multiple_of(c * tq, tq), tq)


def _scores_t(k, q):
    return lax.dot_general(k, q, (((1,), (1,)), ((), ())), preferred_element_type=F32)


def _attn_pipeline(n_heads, n_chunks, load_q, k_of, vt_of, store_o, s_bufs, p_bufs, key_block):
    items = [(h, c) for h in range(n_heads) for c in range(n_chunks)]
    n_kb = s_bufs[0].shape[0] // key_block
    col_max = {}
    for t in range(-2, len(items)):
        qk_i, ex_i, pv_i = t + 2, t + 1, t
        do_qk, do_ex, do_pv = qk_i < len(items), 0 <= ex_i < len(items), pv_i >= 0
        if do_qk:
            k_ref, q = k_of(items[qk_i][0]), load_q(*items[qk_i])
        if do_pv:
            vt_ref = vt_of(items[pv_i][0])
        m_new = acc = None
        for kb in range(n_kb):
            rows = slice(kb * key_block, (kb + 1) * key_block)
            if do_qk:
                s_blk = _scores_t(k_ref[rows, :], q)
                s_bufs[qk_i % 2][rows, :] = s_blk
                bm = jnp.max(s_blk, axis=0, keepdims=True)
                m_new = bm if m_new is None else jnp.maximum(m_new, bm)
            if do_ex:
                s_blk = s_bufs[ex_i % 2][rows, :]
                p_bufs[ex_i % 2][rows, :] = jnp.exp2(s_blk - col_max[ex_i]).astype(BF16)
            if do_pv:
                d = jnp.dot(vt_ref[kb], p_bufs[pv_i % 2][rows, :], preferred_element_type=F32)
                acc = d if acc is None else acc + d
        if do_qk:
            col_max[qk_i] = m_new
        if do_pv:
            dv = acc.shape[0] - ONES_ROWS
            store_o(*items[pv_i], (acc[:dv] * (1.0 / acc[dv:dv + 1])).T)


def _fill_vt(vt_ref, v_lat, v_ctx):
    kb_w = vt_ref.shape[2]
    dv = vt_ref.shape[1] - ONES_ROWS
    blocks = [v_lat[i:i + kb_w] for i in range(0, v_lat.shape[0], kb_w)]
    blocks += [v_ctx[i:i + kb_w] for i in range(0, v_ctx.shape[0], kb_w)]
    for kb, blk in enumerate(blocks):
        vt_ref[kb, :dv, :] = blk.astype(F32).T.astype(BF16)
        vt_ref[kb, dv:, :] = jnp.ones((ONES_ROWS, kb_w), BF16)


def _attn_a_kernel(q_ref, knl_ref, kpl_ref, val_ref, knc_ref, kpc_ref, vac_ref, o_ref,
                   k_scr, vt_scr, s0, s1, p0, p1, *, tq, heads):
    n_lat = knl_ref.shape[1]
    for h in range(heads):
        cols = slice(h * LANES, (h + 1) * LANES)
        k_scr[h, :n_lat, :LANES] = knl_ref[0, :, cols]
        k_scr[h, :n_lat, LANES:] = kpl_ref[0]
        k_scr[h, n_lat:, :LANES] = knc_ref[0, :, cols]
        k_scr[h, n_lat:, LANES:] = kpc_ref[0]
        _fill_vt(vt_scr.at[h], val_ref[0, :, cols], vac_ref[0, :, cols])

    def load_q(h, c):
        return q_ref[0, _rows(c, tq), h * QA_HEAD_W:(h + 1) * QA_HEAD_W]

    def store_o(h, c, o):
        o_ref[0, _rows(c, tq), h * MLA_V:(h + 1) * MLA_V] = o.astype(BF16)

    _attn_pipeline(heads, q_ref.shape[1] // tq, load_q, lambda h: k_scr.at[h],
                   lambda h: vt_scr.at[h], store_o, (s0, s1), (p0, p1), KEY_BLOCK)


def _attn_a_call(qa, kn_l, kpe_l, va_l, kn_c, kpe_c, va_c):
    b, s, _ = qa.shape
    c = kn_c.shape[1]
    tq = min(TQ_ATTN, s)
    hs = HEADS_PER_STEP_A

    def heads(rows, width):
        return pl.BlockSpec((1, rows, hs * width), lambda bi, h: (bi, 0, h))

    def shared(rows):
        return pl.BlockSpec((1, rows, LANES), lambda bi, h: (bi, 0, 0))

    return pl.pallas_call(
        functools.partial(_attn_a_kernel, tq=tq, heads=hs),
        out_shape=jax.ShapeDtypeStruct((b, s, MLA_HEADS * MLA_V), BF16),
        grid=(b, MLA_HEADS // hs),
        in_specs=[heads(s, QA_HEAD_W), heads(s, MLA_NOPE), shared(s), heads(s, MLA_V),
                  heads(c, MLA_NOPE), shared(c), heads(c, MLA_V)],
        out_specs=heads(s, MLA_V),
        scratch_shapes=[pltpu.VMEM((hs, s + c, QA_HEAD_W), BF16),
                        pltpu.VMEM((hs, (s + c) // KEY_BLOCK, MLA_V + ONES_ROWS, KEY_BLOCK), BF16),
                        pltpu.VMEM((s + c, tq), F32), pltpu.VMEM((s + c, tq), F32),
                        pltpu.VMEM((s + c, tq), BF16), pltpu.VMEM((s + c, tq), BF16)],
        compiler_params=_params(2),
        name="attn_a",
    )(qa, kn_l, kpe_l, va_l, kn_c, kpe_c, va_c)


def _attn_b_kernel(q_ref, kl_ref, vl_ref, kc_ref, vc_ref, o_ref, k_scr, vt_scr, s0, s1, p0, p1,
                   *, tq):
    n_lat = kl_ref.shape[1]
    k_scr[:n_lat] = kl_ref[0]
    k_scr[n_lat:] = kc_ref[0]
    hd = GQA_HEAD_DIM
    _fill_vt(vt_scr, vl_ref[0], vc_ref[0])

    def load_q(h, c):
        return q_ref[0, _rows(c, tq), h * hd:(h + 1) * hd]

    def store_o(h, c, o):
        o_ref[0, _rows(c, tq), h * hd:(h + 1) * hd] = o.astype(BF16)

    _attn_pipeline(GQA_HEADS // GQA_KV_HEADS, q_ref.shape[1] // tq, load_q, lambda h: k_scr,
                   lambda h: vt_scr, store_o, (s0, s1), (p0, p1), KEY_BLOCK)


def _attn_b_call(qb, kb_l, vb_l, kb_c, vb_c):
    b, s, _ = qb.shape
    c = kb_c.shape[1]
    tq = min(TQ_ATTN, s)
    gw = (GQA_HEADS // GQA_KV_HEADS) * GQA_HEAD_DIM

    def kv(rows):
        return pl.BlockSpec((1, rows, GQA_HEAD_DIM), lambda bi, g: (bi, 0, g))

    qo = pl.BlockSpec((1, s, gw), lambda bi, g: (bi, 0, g))
    return pl.pallas_call(
        functools.partial(_attn_b_kernel, tq=tq),
        out_shape=jax.ShapeDtypeStruct((b, s, GQA_HEADS * GQA_HEAD_DIM), BF16),
        grid=(b, GQA_KV_HEADS),
        in_specs=[qo, kv(s), kv(s), kv(c), kv(c)],
        out_specs=qo,
        scratch_shapes=[pltpu.VMEM((s + c, GQA_HEAD_DIM), BF16),
                        pltpu.VMEM(((s + c) // KEY_BLOCK, GQA_HEAD_DIM + ONES_ROWS, KEY_BLOCK), BF16),
                        pltpu.VMEM((s + c, tq), F32), pltpu.VMEM((s + c, tq), F32),
                        pltpu.VMEM((s + c, tq), BF16), pltpu.VMEM((s + c, tq), BF16)],
        compiler_params=_params(2),
        name="attn_b",
    )(qb, kb_l, vb_l, kb_c, vb_c)


def _merge_kernel(x_ref, mod_ref, n1g_ref, oa_ref, ob_ref, wga_ref, wgb_ref, wba_ref, wbb_ref,
                  wo_ref, n2g_ref, x1_ref, z2_ref, z_scr, m_scr):
    j = pl.program_id(2)
    nj = pl.num_programs(2)
    mod = mod_ref[0]

    @pl.when(j == 0)
    def _():
        z_scr[...] = (_rms(x_ref[0], n1g_ref[...]) * (1.0 + mod[1:2]) + mod[0:1]).astype(BF16)

    z = z_scr[...]
    ga = _sigmoid(jnp.dot(z, wga_ref[...], preferred_element_type=F32))
    gb = _sigmoid(jnp.dot(z, wgb_ref[...], preferred_element_type=F32))
    ba = jnp.dot(oa_ref[0], wba_ref[...], preferred_element_type=F32)
    bb = jnp.dot(ob_ref[0], wbb_ref[...], preferred_element_type=F32)
    m_scr[j] = (ga * ba + gb * bb).astype(BF16)

    @pl.when(j == nj - 1)
    def _():
        tc = m_scr.shape[2]
        a = jnp.dot(m_scr[0], wo_ref[:tc, :], preferred_element_type=F32)
        for k in range(1, m_scr.shape[0]):
            a = a + jnp.dot(m_scr[k], wo_ref[k * tc:(k + 1) * tc, :], preferred_element_type=F32)
        x1 = x_ref[0] + mod[2:3] * a
        x1_ref[0] = x1
        z2_ref[0] = (_rms(x1, n2g_ref[...]) * (1.0 + mod[4:5]) + mod[3:4]).astype(BF16)


def _merge_call(x, mod3, n1g, oa, ob, wga, wgb, wba, wbb, wo, n2g):
    b, s, d = x.shape
    tm = min(TM_MERGE, s)
    tc = TC_MERGE
    nj = d // tc
    ka, kb = wba.shape[0], wbb.shape[0]

    def tok(width):
        return pl.BlockSpec((1, tm, width), lambda bi, i, j: (bi, i, 0))

    def wcol(rows):
        return pl.BlockSpec((rows, tc), lambda bi, i, j: (0, j))

    return pl.pallas_call(
        _merge_kernel,
        out_shape=[jax.ShapeDtypeStruct((b, s, d), F32), jax.ShapeDtypeStruct((b, s, d), BF16)],
        grid=(b, s // tm, nj),
        in_specs=[tok(d), pl.BlockSpec((1, 6, d), lambda bi, i, j: (bi, 0, 0)), _resident((1, d)),
                  tok(ka), tok(kb), wcol(d), wcol(d), wcol(ka), wcol(kb),
                  _resident((d, d)), _resident((1, d))],
        out_specs=[tok(d), tok(d)],
        scratch_shapes=[pltpu.VMEM((tm, d), BF16), pltpu.VMEM((nj, tm, tc), BF16)],
        compiler_params=_params(3),
        name="merge",
    )(x, mod3, n1g, oa, ob, wga, wgb, wba, wbb, wo, n2g)


def _ffn_kernel(z_ref, zp_ref, zn_ref, x1_ref, mod_ref, wa_ref, wb_ref, cwa_ref, cwb_ref,
                cba_ref, cbb_ref, wd_ref, fg_ref, o_ref, zext, acc, u_even, u_odd, h_even, h_odd,
                *, nf):
    i = pl.program_id(1)
    f = pl.program_id(2)
    tm = z_ref.shape[1]
    halo = zp_ref.shape[1]

    n_pieces = FFN_PIECES
    kc = zext.shape[1] // n_pieces
    rc = tm // n_pieces
    nc = wd_ref.shape[1] // n_pieces
    pad = 8

    def conv(u, cw_ref, cb_ref):
        cw = cw_ref[...]
        n = u.shape[0]
        prev = pltpu.roll(u, 1, 0)[pad:pad + rc]
        nxt = pltpu.roll(u, n - 1, 0)[pad:pad + rc]
        return cb_ref[...] + cw[0:1] * prev + cw[1:2] * u[pad:pad + rc] + cw[2:3] * nxt

    def step(u_up, gate_bufs, h_down):
        ua = ub = None
        for j in range(n_pieces):
            if u_up is not None:
                ze = zext[:, j * kc:(j + 1) * kc]
                da = jnp.dot(ze, wa_ref[j * kc:(j + 1) * kc, :], preferred_element_type=F32)
                db = jnp.dot(ze, wb_ref[j * kc:(j + 1) * kc, :], preferred_element_type=F32)
                ua = da if ua is None else ua + da
                ub = db if ub is None else ub + db
            if gate_bufs is not None:
                u_ref, h_ref = gate_bufs
                lo = halo + j * rc - pad
                ga = conv(u_ref[0, lo:lo + rc + 2 * pad, :], cwa_ref, cba_ref)
                gb = conv(u_ref[1, lo:lo + rc + 2 * pad, :], cwb_ref, cbb_ref)
                h_ref[j * rc:(j + 1) * rc, :] = (ga * _sigmoid(ga) * gb).astype(BF16)
            if h_down is not None:
                cs = slice(j * nc, (j + 1) * nc)
                acc[:, cs] += jnp.dot(h_down[...], wd_ref[:, cs], preferred_element_type=F32)
        if u_up is not None:
            u_up[0] = ua
            u_up[1] = ub

    u_of = lambda t: u_even if t % 2 == 0 else u_odd
    h_of = lambda t: h_even if t % 2 == 0 else h_odd

    @pl.when(f == 0)
    def _():
        zext[:halo] = jnp.where(i > 0, zp_ref[0], jnp.zeros_like(zp_ref[0]))
        zext[halo:halo + tm] = z_ref[0]
        zext[halo + tm:] = jnp.where(i < pl.num_programs(1) - 1, zn_ref[0],
                                     jnp.zeros_like(zn_ref[0]))
        acc[...] = jnp.zeros_like(acc)
        step(u_of(0), None, None)

    @pl.when(f == 1)
    def _():
        step(u_of(1), (u_of(0), h_of(0)), None)

    steady = jnp.logical_and(f >= 2, f < nf)
    for parity in (0, 1):

        @pl.when(jnp.logical_and(steady, f % 2 == parity))
        def _(parity=parity):
            step(u_of(parity), (u_of(parity + 1), h_of(parity + 1)), h_of(parity))

    @pl.when(f == nf)
    def _():
        step(None, (u_of(nf - 1), h_of(nf - 1)), h_of(nf - 2))

    @pl.when(f == nf + 1)
    def _():
        step(None, None, h_of(nf - 1))
        x2 = x1_ref[0] + mod_ref[0][5:6] * acc[...]
        o_ref[0] = _rms(x2, fg_ref[...])


def _ffn_call(z2, x1, mod3, w_up, conv_w, conv_b, w_down, fg):
    b, s, d = x1.shape
    dff = w_down.shape[0]
    tm = min(TM_FFN, s)
    tf = TF_FFN
    nf = dff // tf
    halo = BF16_SUBLANES
    hb = tm // halo
    last_hb = s // halo - 1

    def tok():
        return pl.BlockSpec((1, tm, d), lambda bi, i, f: (bi, i, 0))

    prev = pl.BlockSpec((1, halo, d), lambda bi, i, f: (bi, jnp.maximum(i * hb - 1, 0), 0))
    nxt = pl.BlockSpec((1, halo, d), lambda bi, i, f: (bi, jnp.minimum((i + 1) * hb, last_hb), 0))

    def up_cols(rows, second, lag):
        off = nf if second else 0
        return pl.BlockSpec((rows, tf), lambda bi, i, f: (0, jnp.clip(f - lag, 0, nf - 1) + off))

    n_ext = tm + 2 * halo
    return pl.pallas_call(
        functools.partial(_ffn_kernel, nf=nf),
        out_shape=jax.ShapeDtypeStruct((b, s, d), F32),
        grid=(b, s // tm, nf + 2),
        in_specs=[tok(), prev, nxt, tok(),
                  pl.BlockSpec((1, 6, d), lambda bi, i, f: (bi, 0, 0)),
                  up_cols(d, False, 0), up_cols(d, True, 0),
                  up_cols(CONV_W, False, 1), up_cols(CONV_W, True, 1),
                  up_cols(1, False, 1), up_cols(1, True, 1),
                  pl.BlockSpec((tf, d), lambda bi, i, f: (jnp.clip(f - 2, 0, nf - 1), 0)),
                  _resident((1, d))],
        out_specs=tok(),
        scratch_shapes=[pltpu.VMEM((n_ext, d), BF16), pltpu.VMEM((tm, d), F32),
                        pltpu.VMEM((2, n_ext, tf), F32), pltpu.VMEM((2, n_ext, tf), F32),
                        pltpu.VMEM((tm, tf), BF16), pltpu.VMEM((tm, tf), BF16)],
        compiler_params=_params(3),
        name="ffn",
    )(z2, z2, z2, x1, mod3, w_up, w_up, conv_w, conv_w, conv_b, conv_b, w_down, fg)


def _rope_table(n_rows, rot_dim):
    row = jnp.repeat(jnp.arange(n_rows, dtype=F32), GRID_W)
    col = jnp.tile(jnp.arange(GRID_W, dtype=F32), n_rows)
    half = rot_dim // 2
    inv_freq = ROPE_THETA ** (-jnp.arange(0, half, 2, dtype=F32) / half)
    t = row.shape[0]
    quarter = rot_dim // 4
    ang = jnp.stack([row[:, None] * inv_freq, col[:, None] * inv_freq], axis=1)
    cos = jnp.broadcast_to(jnp.cos(ang)[:, :, None, :], (t, 2, 2, quarter))
    sin = jnp.sin(ang)
    sin = jnp.stack([-sin, sin], axis=2)
    pad = ((0, 0), (0, LANES - rot_dim))
    return jnp.pad(cos.reshape(t, rot_dim), pad), jnp.pad(sin.reshape(t, rot_dim), pad)


def kernel(x, c, ctx, c_ctx, w_ada, b_ada, norm1_g, w_in, mla_q_norm_g, w_q_up, mla_kv_norm_g,
           w_kv_up, gqa_q_norm_g, gqa_k_norm_g, w_br_a, w_br_b, w_out, norm2_g, w_up, conv_w,
           conv_b, w_down, final_norm_g):
    bsz, seq, d = x.shape
    n_ctx = ctx.shape[1]
    depth = w_in.shape[0]
    assert depth == 1 and seq % GRID_W == 0 and bsz < ADA_ROWS

    kv_lora, rope_a = MLA_KV_LORA, MLA_ROPE
    kb_w = GQA_KV_HEADS * GQA_HEAD_DIM
    kv_cols = kv_lora + rope_a + 2 * kb_w
    q_cols = MLA_Q_LORA + GQA_HEADS * GQA_HEAD_DIM

    wi = w_in[0]

    def cols(lo, hi):
        return wi[:, lo:hi].astype(BF16)

    c_kv = cols(0, kv_lora)
    k_pe = cols(kv_lora, kv_lora + rope_a)
    k_b = cols(kv_lora + rope_a, kv_lora + rope_a + kb_w)
    v_b = cols(kv_lora + rope_a + kb_w, kv_cols)
    q_part = cols(kv_cols, kv_cols + q_cols)
    w_small = jnp.concatenate(
        [c_kv, k_b, v_b, k_pe, jnp.zeros((d, LANES - rope_a), BF16), q_part], axis=1)
    w_ga = cols(kv_cols + q_cols, kv_cols + q_cols + d)
    w_gb = cols(kv_cols + q_cols + d, kv_cols + q_cols + 2 * d)

    wkv = w_kv_up[0].reshape(kv_lora, MLA_HEADS, 2, MLA_NOPE).transpose(0, 2, 1, 3)
    wkv = wkv.reshape(kv_lora, 2 * MLA_HEADS * MLA_NOPE).astype(BF16)
    wq = w_q_up[0].reshape(MLA_Q_LORA, MLA_HEADS, MLA_NOPE + MLA_ROPE)
    wq = jnp.pad(wq, ((0, 0), (0, 0), (0, QA_HEAD_W - MLA_NOPE - MLA_ROPE)))
    wq = wq.reshape(MLA_Q_LORA, MLA_HEADS * QA_HEAD_W).astype(BF16)

    rope_tabs = (*_rope_table(seq // GRID_W, MLA_ROPE), *_rope_table(seq // GRID_W, GQA_HEAD_DIM))

    cond = jnp.zeros((ADA_ROWS, d), F32).at[:bsz].set(c).at[bsz].set(c_ctx)
    mod3 = _ada_call(cond, w_ada[0], b_ada).reshape(ADA_ROWS, 6, d)

    n1g = norm1_g
    gkv, gkb = mla_kv_norm_g, gqa_k_norm_g
    gq, gqb = mla_q_norm_g, gqa_q_norm_g

    kn_c, va_c, kpe_c, kb_c, vb_c = _kvq_call(
        ctx, mod3, lambda bi: bsz, n1g, w_small[:, :_KV_END], gkv, wkv, gkb, None, None,
        min(TM_KVQ, n_ctx))
    kn_l, va_l, kpe_l, kb_l, vb_l, qa, qb = _kvq_call(
        x, mod3, lambda bi: bi, n1g, w_small, gkv, wkv, gkb, (gq, wq, gqb), rope_tabs,
        min(TM_KVQ, seq))

    oa = _attn_a_call(qa, kn_l, kpe_l, va_l, kn_c, kpe_c, va_c)
    ob = _attn_b_call(qb, kb_l, vb_l, kb_c, vb_c)

    x1, z2 = _merge_call(x, mod3, n1g, oa, ob, w_ga, w_gb, w_br_a[0].astype(BF16),
                         w_br_b[0].astype(BF16), w_out[0].astype(BF16), norm2_g)
    return _ffn_call(z2, x1, mod3, w_up[0].astype(BF16), conv_w[0], conv_b, w_down[0].astype(BF16),
                     final_norm_g.reshape(1, d))
```

```python
import functools

import jax
import jax.numpy as jnp
from jax import lax
from jax.experimental import pallas as pl
from jax.experimental.pallas import tpu as pltpu

GRID_W = 64
ROPE_THETA = 10000.0
NORM_EPS = 1e-6
LOG2_E = 1.4426950408889634
MLA_HEADS = 8
MLA_Q_LORA = 768
MLA_KV_LORA = 512
MLA_NOPE = 128
MLA_ROPE = 64
MLA_V = 128
GQA_HEADS = 8
GQA_KV_HEADS = 2
GQA_HEAD_DIM = 128
CONV_W = 3
N_BRANCH = 2

LANES = 128
BF16_SUBLANES = 16
VMEM_LIMIT_BYTES = 56 * 1024 * 1024

TM_KVQ = 512
TM_MERGE = 512
TC_MERGE = 512
TM_FFN = 512
TF_FFN = 512
TQ_ATTN = 512
HEADS_PER_STEP_A = 2
KEY_BLOCK = 256
ONES_ROWS = BF16_SUBLANES
TN_ADA = 1536
ADA_ROWS = 16

F32 = jnp.float32
BF16 = jnp.bfloat16

_C_KV = 0
_K_B = _C_KV + MLA_KV_LORA
_V_B = _K_B + GQA_KV_HEADS * GQA_HEAD_DIM
_K_PE = _V_B + GQA_KV_HEADS * GQA_HEAD_DIM
_KV_END = _K_PE + LANES
_C_Q = _KV_END
_Q_B = _C_Q + MLA_Q_LORA
_Q_END = _Q_B + GQA_HEADS * GQA_HEAD_DIM
QA_HEAD_W = 2 * LANES


def _params(n_axes, flags=None):
    return pltpu.CompilerParams(
        dimension_semantics=("arbitrary",) * n_axes,
        vmem_limit_bytes=VMEM_LIMIT_BYTES,
        flags=flags,
    )


def _resident(shape):
    zeros = (0,) * len(shape)
    return pl.BlockSpec(shape, lambda *_: zeros, pipeline_mode=pl.Buffered(1))


def _rms(x, g):
    return x * lax.rsqrt(jnp.mean(x * x, axis=-1, keepdims=True) + NORM_EPS) * g


def _rope(t, cos, sin_signed, quarter):
    lane = lax.broadcasted_iota(jnp.int32, t.shape, 1)
    first = (lane % (2 * quarter)) < quarter
    partner = jnp.where(first, pltpu.roll(t, LANES - quarter, 1), pltpu.roll(t, quarter, 1))
    return t * cos + partner * sin_signed


def _sigmoid(x):
    return 1.0 / (1.0 + jnp.exp(-x))


def _ada_kernel(c_ref, w_ref, b_ref, o_ref):
    c = c_ref[...]
    s = (c * _sigmoid(c)).astype(BF16)
    o_ref[...] = jnp.dot(s, w_ref[...].astype(BF16), preferred_element_type=F32) + b_ref[...]


def _ada_call(cond, w_ada, b_ada):
    d, n = w_ada.shape
    return pl.pallas_call(
        _ada_kernel,
        out_shape=jax.ShapeDtypeStruct((ADA_ROWS, n), F32),
        grid=(n // TN_ADA,),
        in_specs=[
            pl.BlockSpec((ADA_ROWS, d), lambda j: (0, 0)),
            pl.BlockSpec((d, TN_ADA), lambda j: (0, j)),
            pl.BlockSpec((1, TN_ADA), lambda j: (0, j)),
        ],
        out_specs=pl.BlockSpec((ADA_ROWS, TN_ADA), lambda j: (0, j)),
        compiler_params=_params(1),
        name="ada",
    )(cond, w_ada, b_ada)


def _kvq_kernel(*refs, with_q, with_rope):
    it = iter(refs)
    x_ref, mod_ref, n1g_ref, w_ref, gkv_ref, wkv_ref, gkb_ref = (next(it) for _ in range(7))
    if with_q:
        gq_ref, wq_ref, gqb_ref = (next(it) for _ in range(3))
    if with_rope:
        ca_ref, sa_ref, cb_ref, sb_ref = (next(it) for _ in range(4))
    kn_ref, va_ref, kpe_ref, kb_ref, vb_ref = (next(it) for _ in range(5))
    if with_q:
        qa_ref, qb_ref = (next(it) for _ in range(2))

    mod = mod_ref[0]
    z = (_rms(x_ref[0], n1g_ref[...]) * (1.0 + mod[1:2]) + mod[0:1]).astype(BF16)
    proj = jnp.dot(z, w_ref[...], preferred_element_type=F32)

    if with_rope:
        ca, sa, cb, sb = ca_ref[...], sa_ref[...], cb_ref[...], sb_ref[...]

    ckv = _rms(proj[:, _C_KV:_K_B], gkv_ref[...]).astype(BF16)
    kv_up = jnp.dot(ckv, wkv_ref[...], preferred_element_type=F32)
    hn = MLA_HEADS * MLA_NOPE
    kn_ref[0] = kv_up[:, :hn].astype(BF16)
    va_ref[0] = kv_up[:, hn:].astype(BF16)

    kpe = proj[:, _K_PE:_KV_END]
    if with_rope:
        kpe = _rope(kpe, ca, sa, MLA_ROPE // 4)
    kpe_ref[0] = kpe.astype(BF16)

    for h in range(GQA_KV_HEADS):
        lo = _K_B + h * GQA_HEAD_DIM
        t = _rms(proj[:, lo:lo + GQA_HEAD_DIM], gkb_ref[...])
        if with_rope:
            t = _rope(t, cb, sb, GQA_HEAD_DIM // 4)
        kb_ref[0, :, h * GQA_HEAD_DIM:(h + 1) * GQA_HEAD_DIM] = t.astype(BF16)
    vb_ref[0] = proj[:, _V_B:_K_PE].astype(BF16)

    if with_q:
        scale_a = LOG2_E * float(MLA_NOPE + MLA_ROPE) ** -0.5
        scale_b = LOG2_E * float(GQA_HEAD_DIM) ** -0.5
        cq = _rms(proj[:, _C_Q:_Q_B], gq_ref[...]).astype(BF16)
        q_up = jnp.dot(cq, wq_ref[...], preferred_element_type=F32)
        for h in range(MLA_HEADS):
            lo = h * QA_HEAD_W
            qa_ref[0, :, lo:lo + LANES] = (q_up[:, lo:lo + LANES] * scale_a).astype(BF16)
            pe = q_up[:, lo + LANES:lo + QA_HEAD_W]
            if with_rope:
                pe = _rope(pe, ca, sa, MLA_ROPE // 4)
            qa_ref[0, :, lo + LANES:lo + QA_HEAD_W] = (pe * scale_a).astype(BF16)
        for h in range(GQA_HEADS):
            lo = _Q_B + h * GQA_HEAD_DIM
            t = _rms(proj[:, lo:lo + GQA_HEAD_DIM], gqb_ref[...])
            if with_rope:
                t = _rope(t, cb, sb, GQA_HEAD_DIM // 4)
            qb_ref[0, :, h * GQA_HEAD_DIM:(h + 1) * GQA_HEAD_DIM] = (t * scale_b).astype(BF16)


def _kvq_call(x, mod3, mod_row, n1g, w_small, gkv, wkv, gkb, q_parts, rope_tabs, tm):
    b, t, d = x.shape
    with_q = q_parts is not None
    with_rope = rope_tabs is not None
    nt = t // tm
    cols = w_small.shape[1]

    in_specs = [
        pl.BlockSpec((1, tm, d), lambda bi, i: (bi, i, 0)),
        pl.BlockSpec((1, 6, d), lambda bi, i: (mod_row(bi), 0, 0)),
        _resident((1, d)),
        _resident((d, cols)),
        _resident(gkv.shape),
        _resident(wkv.shape),
        _resident(gkb.shape),
    ]
    args = [x, mod3, n1g, w_small, gkv, wkv, gkb]
    if with_q:
        gq, wq, gqb = q_parts
        in_specs += [_resident(gq.shape), _resident(wq.shape), _resident(gqb.shape)]
        args += [gq, wq, gqb]
    if with_rope:
        in_specs += [pl.BlockSpec((tm, LANES), lambda bi, i: (i, 0))] * 4
        args += list(rope_tabs)

    def tok(width):
        return pl.BlockSpec((1, tm, width), lambda bi, i: (bi, i, 0))

    widths = [MLA_HEADS * MLA_NOPE, MLA_HEADS * MLA_V, LANES,
              GQA_KV_HEADS * GQA_HEAD_DIM, GQA_KV_HEADS * GQA_HEAD_DIM]
    if with_q:
        widths += [MLA_HEADS * QA_HEAD_W, GQA_HEADS * GQA_HEAD_DIM]
    return pl.pallas_call(
        functools.partial(_kvq_kernel, with_q=with_q, with_rope=with_rope),
        out_shape=[jax.ShapeDtypeStruct((b, t, w), BF16) for w in widths],
        grid=(b, nt),
        in_specs=in_specs,
        out_specs=[tok(w) for w in widths],
        compiler_params=_params(2),
        name="kvq_lat" if with_q else "kvq_ctx",
    )(*args)


def _rows(c, tq):
    if isinstance(c, int):
        return pl.ds(c * tq, tq)
    return pl.ds(pl.multiple_of(c * tq, tq), tq)


def _scores_t(k, q):
    return lax.dot_general(k, q, (((1,), (1,)), ((), ())), preferred_element_type=F32)


def _attn_pipeline(n_heads, n_chunks, load_q, k_of, vt_of, store_o, s_bufs, p_bufs, key_block):
    items = [(h, c) for h in range(n_heads) for c in range(n_chunks)]
    n_kb = s_bufs[0].shape[0] // key_block
    col_max = {}
    for t in range(-2, len(items)):
        qk_i, ex_i, pv_i = t + 2, t + 1, t
        do_qk, do_ex, do_pv = qk_i < len(items), 0 <= ex_i < len(items), pv_i >= 0
        if do_qk:
            k_ref, q = k_of(items[qk_i][0]), load_q(*items[qk_i])
        if do_pv:
            vt_ref = vt_of(items[pv_i][0])
        m_new = acc = None
        for kb in range(n_kb):
            rows = slice(kb * key_block, (kb + 1) * key_block)
            if do_qk:
                s_blk = _scores_t(k_ref[rows, :], q)
                s_bufs[qk_i % 2][rows, :] = s_blk
                bm = jnp.max(s_blk, axis=0, keepdims=True)
                m_new = bm if m_new is None else jnp.maximum(m_new, bm)
            if do_ex:
                s_blk = s_bufs[ex_i % 2][rows, :]
                p_bufs[ex_i % 2][rows, :] = jnp.exp2(s_blk - col_max[ex_i]).astype(BF16)
            if do_pv:
                d = jnp.dot(vt_ref[kb], p_bufs[pv_i % 2][rows, :], preferred_element_type=F32)
                acc = d if acc is None else acc + d
        if do_qk:
            col_max[qk_i] = m_new
        if do_pv:
            dv = acc.shape[0] - ONES_ROWS
            store_o(*items[pv_i], (acc[:dv] * (1.0 / acc[dv:dv + 1])).T)


def _fill_vt(vt_ref, v_lat, v_ctx):
    kb_w = vt_ref.shape[2]
    dv = vt_ref.shape[1] - ONES_ROWS
    blocks = [v_lat[i:i + kb_w] for i in range(0, v_lat.shape[0], kb_w)]
    blocks += [v_ctx[i:i + kb_w] for i in range(0, v_ctx.shape[0], kb_w)]
    for kb, blk in enumerate(blocks):
        vt_ref[kb, :dv, :] = blk.astype(F32).T.astype(BF16)
        vt_ref[kb, dv:, :] = jnp.ones((ONES_ROWS, kb_w), BF16)


def _attn_a_kernel(q_ref, knl_ref, kpl_ref, val_ref, knc_ref, kpc_ref, vac_ref, o_ref,
                   k_scr, vt_scr, s0, s1, p0, p1, *, tq, heads):
    n_lat = knl_ref.shape[1]
    for h in range(heads):
        cols = slice(h * LANES, (h + 1) * LANES)
        k_scr[h, :n_lat, :LANES] = knl_ref[0, :, cols]
        k_scr[h, :n_lat, LANES:] = kpl_ref[0]
        k_scr[h, n_lat:, :LANES] = knc_ref[0, :, cols]
        k_scr[h, n_lat:, LANES:] = kpc_ref[0]
        _fill_vt(vt_scr.at[h], val_ref[0, :, cols], vac_ref[0, :, cols])

    def load_q(h, c):
        return q_ref[0, _rows(c, tq), h * QA_HEAD_W:(h + 1) * QA_HEAD_W]

    def store_o(h, c, o):
        o_ref[0, _rows(c, tq), h * MLA_V:(h + 1) * MLA_V] = o.astype(BF16)

    _attn_pipeline(heads, q_ref.shape[1] // tq, load_q, lambda h: k_scr.at[h],
                   lambda h: vt_scr.at[h], store_o, (s0, s1), (p0, p1), KEY_BLOCK)


def _attn_a_call(qa, kn_l, kpe_l, va_l, kn_c, kpe_c, va_c):
    b, s, _ = qa.shape
    c = kn_c.shape[1]
    tq = min(TQ_ATTN, s)
    hs = HEADS_PER_STEP_A

    def heads(rows, width):
        return pl.BlockSpec((1, rows, hs * width), lambda bi, h: (bi, 0, h))

    def shared(rows):
        return pl.BlockSpec((1, rows, LANES), lambda bi, h: (bi, 0, 0))

    return pl.pallas_call(
        functools.partial(_attn_a_kernel, tq=tq, heads=hs),
        out_shape=jax.ShapeDtypeStruct((b, s, MLA_HEADS * MLA_V), BF16),
        grid=(b, MLA_HEADS // hs),
        in_specs=[heads(s, QA_HEAD_W), heads(s, MLA_NOPE), shared(s), heads(s, MLA_V),
                  heads(c, MLA_NOPE), shared(c), heads(c, MLA_V)],
        out_specs=heads(s, MLA_V),
        scratch_shapes=[pltpu.VMEM((hs, s + c, QA_HEAD_W), BF16),
                        pltpu.VMEM((hs, (s + c) // KEY_BLOCK, MLA_V + ONES_ROWS, KEY_BLOCK), BF16),
                        pltpu.VMEM((s + c, tq), F32), pltpu.VMEM((s + c, tq), F32),
                        pltpu.VMEM((s + c, tq), BF16), pltpu.VMEM((s + c, tq), BF16)],
        compiler_params=_params(2),
        name="attn_a",
    )(qa, kn_l, kpe_l, va_l, kn_c, kpe_c, va_c)


def _attn_b_kernel(q_ref, kl_ref, vl_ref, kc_ref, vc_ref, o_ref, k_scr, vt_scr, s0, s1, p0, p1,
                   *, tq):
    n_lat = kl_ref.shape[1]
    k_scr[:n_lat] = kl_ref[0]
    k_scr[n_lat:] = kc_ref[0]
    hd = GQA_HEAD_DIM
    _fill_vt(vt_scr, vl_ref[0], vc_ref[0])

    def load_q(h, c):
        return q_ref[0, _rows(c, tq), h * hd:(h + 1) * hd]

    def store_o(h, c, o):
        o_ref[0, _rows(c, tq), h * hd:(h + 1) * hd] = o.astype(BF16)

    _attn_pipeline(GQA_HEADS // GQA_KV_HEADS, q_ref.shape[1] // tq, load_q, lambda h: k_scr,
                   lambda h: vt_scr, store_o, (s0, s1), (p0, p1), KEY_BLOCK)


def _attn_b_call(qb, kb_l, vb_l, kb_c, vb_c):
    b, s, _ = qb.shape
    c = kb_c.shape[1]
    tq = min(TQ_ATTN, s)
    gw = (GQA_HEADS // GQA_KV_HEADS) * GQA_HEAD_DIM

    def kv(rows):
        return pl.BlockSpec((1, rows, GQA_HEAD_DIM), lambda bi, g: (bi, 0, g))

    qo = pl.BlockSpec((1, s, gw), lambda bi, g: (bi, 0, g))
    return pl.pallas_call(
        functools.partial(_attn_b_kernel, tq=tq),
        out_shape=jax.ShapeDtypeStruct((b, s, GQA_HEADS * GQA_HEAD_DIM), BF16),
        grid=(b, GQA_KV_HEADS),
        in_specs=[qo, kv(s), kv(s), kv(c), kv(c)],
        out_specs=qo,
        scratch_shapes=[pltpu.VMEM((s + c, GQA_HEAD_DIM), BF16),
                        pltpu.VMEM(((s + c) // KEY_BLOCK, GQA_HEAD_DIM + ONES_ROWS, KEY_BLOCK), BF16),
                        pltpu.VMEM((s + c, tq), F32), pltpu.VMEM((s + c, tq), F32),
                        pltpu.VMEM((s + c, tq), BF16), pltpu.VMEM((s + c, tq), BF16)],
        compiler_params=_params(2),
        name="attn_b",
    )(qb, kb_l, vb_l, kb_c, vb_c)


def _merge_kernel(x_ref, mod_ref, n1g_ref, oa_ref, ob_ref, wga_ref, wgb_ref, wba_ref, wbb_ref,
                  wo_ref, n2g_ref, x1_ref, z2_ref, z_scr, m_scr):
    j = pl.program_id(2)
    nj = pl.num_programs(2)
    mod = mod_ref[0]

    @pl.when(j == 0)
    def _():
        z_scr[...] = (_rms(x_ref[0], n1g_ref[...]) * (1.0 + mod[1:2]) + mod[0:1]).astype(BF16)

    z = z_scr[...]
    ga = _sigmoid(jnp.dot(z, wga_ref[...], preferred_element_type=F32))
    gb = _sigmoid(jnp.dot(z, wgb_ref[...], preferred_element_type=F32))
    ba = jnp.dot(oa_ref[0], wba_ref[...], preferred_element_type=F32)
    bb = jnp.dot(ob_ref[0], wbb_ref[...], preferred_element_type=F32)
    m_scr[j] = (ga * ba + gb * bb).astype(BF16)

    @pl.when(j == nj - 1)
    def _():
        tc = m_scr.shape[2]
        a = jnp.dot(m_scr[0], wo_ref[:tc, :], preferred_element_type=F32)
        for k in range(1, m_scr.shape[0]):
            a = a + jnp.dot(m_scr[k], wo_ref[k * tc:(k + 1) * tc, :], preferred_element_type=F32)
        x1 = x_ref[0] + mod[2:3] * a
        x1_ref[0] = x1
        z2_ref[0] = (_rms(x1, n2g_ref[...]) * (1.0 + mod[4:5]) + mod[3:4]).astype(BF16)


def _merge_call(x, mod3, n1g, oa, ob, wga, wgb, wba, wbb, wo, n2g):
    b, s, d = x.shape
    tm = min(TM_MERGE, s)
    tc = TC_MERGE
    nj = d // tc
    ka, kb = wba.shape[0], wbb.shape[0]

    def tok(width):
        return pl.BlockSpec((1, tm, width), lambda bi, i, j: (bi, i, 0))

    def wcol(rows):
        return pl.BlockSpec((rows, tc), lambda bi, i, j: (0, j))

    return pl.pallas_call(
        _merge_kernel,
        out_shape=[jax.ShapeDtypeStruct((b, s, d), F32), jax.ShapeDtypeStruct((b, s, d), BF16)],
        grid=(b, s // tm, nj),
        in_specs=[tok(d), pl.BlockSpec((1, 6, d), lambda bi, i, j: (bi, 0, 0)), _resident((1, d)),
                  tok(ka), tok(kb), wcol(d), wcol(d), wcol(ka), wcol(kb),
                  _resident((d, d)), _resident((1, d))],
        out_specs=[tok(d), tok(d)],
        scratch_shapes=[pltpu.VMEM((tm, d), BF16), pltpu.VMEM((nj, tm, tc), BF16)],
        compiler_params=_params(3),
        name="merge",
    )(x, mod3, n1g, oa, ob, wga, wgb, wba, wbb, wo, n2g)


def _ffn_kernel(z_ref, zp_ref, zn_ref, x1_ref, mod_ref, wa_ref, wb_ref, cwa_ref, cwb_ref,
                cba_ref, cbb_ref, wd_ref, fg_ref, o_ref, zext, acc):
    i = pl.program_id(1)
    f = pl.program_id(2)
    tm = z_ref.shape[1]
    halo = zp_ref.shape[1]
    n_ext = zext.shape[0]

    @pl.when(f == 0)
    def _():
        zext[:halo] = jnp.where(i > 0, zp_ref[0], jnp.zeros_like(zp_ref[0]))
        zext[halo:halo + tm] = z_ref[0]
        zext[halo + tm:] = jnp.where(i < pl.num_programs(1) - 1, zn_ref[0],
                                     jnp.zeros_like(zn_ref[0]))
        acc[...] = jnp.zeros_like(acc)

    def conv(u, cw_ref, cb_ref):
        cw = cw_ref[...]
        prev = pltpu.roll(u, 1, 0)[halo:halo + tm]
        nxt = pltpu.roll(u, n_ext - 1, 0)[halo:halo + tm]
        return cb_ref[...] + cw[0:1] * prev + cw[1:2] * u[halo:halo + tm] + cw[2:3] * nxt

    for parity in (0, 1):

        @pl.when(f % 2 == parity)
        def _():
            ze = zext[...]
            ua = conv(jnp.dot(ze, wa_ref[...], preferred_element_type=F32), cwa_ref, cba_ref)
            ub = conv(jnp.dot(ze, wb_ref[...], preferred_element_type=F32), cwb_ref, cbb_ref)
            h = (ua * _sigmoid(ua) * ub).astype(BF16)
            acc[...] += jnp.dot(h, wd_ref[...], preferred_element_type=F32)

    @pl.when(f == pl.num_programs(2) - 1)
    def _():
        x2 = x1_ref[0] + mod_ref[0][5:6] * acc[...]
        o_ref[0] = _rms(x2, fg_ref[...])


def _ffn_call(z2, x1, mod3, w_up, conv_w, conv_b, w_down, fg):
    b, s, d = x1.shape
    dff = w_down.shape[0]
    tm = min(TM_FFN, s)
    tf = TF_FFN
    nf = dff // tf
    halo = BF16_SUBLANES
    hb = tm // halo
    last_hb = s // halo - 1

    def tok():
        return pl.BlockSpec((1, tm, d), lambda bi, i, f: (bi, i, 0))

    prev = pl.BlockSpec((1, halo, d), lambda bi, i, f: (bi, jnp.maximum(i * hb - 1, 0), 0))
    nxt = pl.BlockSpec((1, halo, d), lambda bi, i, f: (bi, jnp.minimum((i + 1) * hb, last_hb), 0))

    def up_cols(rows, second):
        off = nf if second else 0
        return pl.BlockSpec((rows, tf), lambda bi, i, f: (0, f + off))

    return pl.pallas_call(
        _ffn_kernel,
        out_shape=jax.ShapeDtypeStruct((b, s, d), F32),
        grid=(b, s // tm, nf),
        in_specs=[tok(), prev, nxt, tok(),
                  pl.BlockSpec((1, 6, d), lambda bi, i, f: (bi, 0, 0)),
                  up_cols(d, False), up_cols(d, True),
                  up_cols(CONV_W, False), up_cols(CONV_W, True),
                  up_cols(1, False), up_cols(1, True),
                  pl.BlockSpec((tf, d), lambda bi, i, f: (f, 0)),
                  _resident((1, d))],
        out_specs=tok(),
        scratch_shapes=[pltpu.VMEM((tm + 2 * halo, d), BF16), pltpu.VMEM((tm, d), F32)],
        compiler_params=_params(3),
        name="ffn",
    )(z2, z2, z2, x1, mod3, w_up, w_up, conv_w, conv_w, conv_b, conv_b, w_down, fg)


def _rope_table(n_rows, rot_dim):
    row = jnp.repeat(jnp.arange(n_rows, dtype=F32), GRID_W)
    col = jnp.tile(jnp.arange(GRID_W, dtype=F32), n_rows)
    half = rot_dim // 2
    inv_freq = ROPE_THETA ** (-jnp.arange(0, half, 2, dtype=F32) / half)
    t = row.shape[0]
    quarter = rot_dim // 4
    ang = jnp.stack([row[:, None] * inv_freq, col[:, None] * inv_freq], axis=1)
    cos = jnp.broadcast_to(jnp.cos(ang)[:, :, None, :], (t, 2, 2, quarter))
    sin = jnp.sin(ang)
    sin = jnp.stack([-sin, sin], axis=2)
    pad = ((0, 0), (0, LANES - rot_dim))
    return jnp.pad(cos.reshape(t, rot_dim), pad), jnp.pad(sin.reshape(t, rot_dim), pad)


def kernel(x, c, ctx, c_ctx, w_ada, b_ada, norm1_g, w_in, mla_q_norm_g, w_q_up, mla_kv_norm_g,
           w_kv_up, gqa_q_norm_g, gqa_k_norm_g, w_br_a, w_br_b, w_out, norm2_g, w_up, conv_w,
           conv_b, w_down, final_norm_g):
    bsz, seq, d = x.shape
    n_ctx = ctx.shape[1]
    depth = w_in.shape[0]
    assert depth == 1 and seq % GRID_W == 0 and bsz < ADA_ROWS

    kv_lora, rope_a = MLA_KV_LORA, MLA_ROPE
    kb_w = GQA_KV_HEADS * GQA_HEAD_DIM
    kv_cols = kv_lora + rope_a + 2 * kb_w
    q_cols = MLA_Q_LORA + GQA_HEADS * GQA_HEAD_DIM

    wi = w_in[0]

    def cols(lo, hi):
        return wi[:, lo:hi].astype(BF16)

    c_kv = cols(0, kv_lora)
    k_pe = cols(kv_lora, kv_lora + rope_a)
    k_b = cols(kv_lora + rope_a, kv_lora + rope_a + kb_w)
    v_b = cols(kv_lora + rope_a + kb_w, kv_cols)
    q_part = cols(kv_cols, kv_cols + q_cols)
    w_small = jnp.concatenate(
        [c_kv, k_b, v_b, k_pe, jnp.zeros((d, LANES - rope_a), BF16), q_part], axis=1)
    w_ga = cols(kv_cols + q_cols, kv_cols + q_cols + d)
    w_gb = cols(kv_cols + q_cols + d, kv_cols + q_cols + 2 * d)

    wkv = w_kv_up[0].reshape(kv_lora, MLA_HEADS, 2, MLA_NOPE).transpose(0, 2, 1, 3)
    wkv = wkv.reshape(kv_lora, 2 * MLA_HEADS * MLA_NOPE).astype(BF16)
    wq = w_q_up[0].reshape(MLA_Q_LORA, MLA_HEADS, MLA_NOPE + MLA_ROPE)
    wq = jnp.pad(wq, ((0, 0), (0, 0), (0, QA_HEAD_W - MLA_NOPE - MLA_ROPE)))
    wq = wq.reshape(MLA_Q_LORA, MLA_HEADS * QA_HEAD_W).astype(BF16)

    rope_tabs = (*_rope_table(seq // GRID_W, MLA_ROPE), *_rope_table(seq // GRID_W, GQA_HEAD_DIM))

    cond = jnp.zeros((ADA_ROWS, d), F32).at[:bsz].set(c).at[bsz].set(c_ctx)
    mod3 = _ada_call(cond, w_ada[0], b_ada).reshape(ADA_ROWS, 6, d)

    n1g = norm1_g
    gkv, gkb = mla_kv_norm_g, gqa_k_norm_g
    gq, gqb = mla_q_norm_g, gqa_q_norm_g

    kn_c, va_c, kpe_c, kb_c, vb_c = _kvq_call(
        ctx, mod3, lambda bi: bsz, n1g, w_small[:, :_KV_END], gkv, wkv, gkb, None, None,
        min(TM_KVQ, n_ctx))
    kn_l, va_l, kpe_l, kb_l, vb_l, qa, qb = _kvq_call(
        x, mod3, lambda bi: bi, n1g, w_small, gkv, wkv, gkb, (gq, wq, gqb), rope_tabs,
        min(TM_KVQ, seq))

    oa = _attn_a_call(qa, kn_l, kpe_l, va_l, kn_c, kpe_c, va_c)
    ob = _attn_b_call(qb, kb_l, vb_l, kb_c, vb_c)

    x1, z2 = _merge_call(x, mod3, n1g, oa, ob, w_ga, w_gb, w_br_a[0].astype(BF16),
                         w_br_b[0].astype(BF16), w_out[0].astype(BF16), norm2_g)
    return _ffn_call(z2, x1, mod3, w_up[0].astype(BF16), conv_w[0], conv_b, w_down[0].astype(BF16),
                     final_norm_g.reshape(1, d))
```

```python
import functools

import jax
import jax.numpy as jnp
from jax import lax
from jax.experimental import pallas as pl
from jax.experimental.pallas import tpu as pltpu

GRID_W = 64
ROPE_THETA = 10000.0
NORM_EPS = 1e-6
LOG2_E = 1.4426950408889634
MLA_HEADS = 8
MLA_Q_LORA = 768
MLA_KV_LORA = 512
MLA_NOPE = 128
MLA_ROPE = 64
MLA_V = 128
GQA_HEADS = 8
GQA_KV_HEADS = 2
GQA_HEAD_DIM = 128
CONV_W = 3
N_BRANCH = 2

LANES = 128
BF16_SUBLANES = 16
VMEM_LIMIT_BYTES = 56 * 1024 * 1024

TM_KVQ = 512
TM_MERGE = 512
TC_MERGE = 512
TM_FFN = 512
TF_FFN = 512
TQ_ATTN = 512
HEADS_PER_STEP_A = 2
KEY_BLOCK = 256
ONES_ROWS = BF16_SUBLANES
TN_ADA = 1536
ADA_ROWS = 16

F32 = jnp.float32
BF16 = jnp.bfloat16

_C_KV = 0
_K_B = _C_KV + MLA_KV_LORA
_V_B = _K_B + GQA_KV_HEADS * GQA_HEAD_DIM
_K_PE = _V_B + GQA_KV_HEADS * GQA_HEAD_DIM
_KV_END = _K_PE + LANES
_C_Q = _KV_END
_Q_B = _C_Q + MLA_Q_LORA
_Q_END = _Q_B + GQA_HEADS * GQA_HEAD_DIM
QA_HEAD_W = 2 * LANES


def _params(n_axes, flags=None):
    return pltpu.CompilerParams(
        dimension_semantics=("arbitrary",) * n_axes,
        vmem_limit_bytes=VMEM_LIMIT_BYTES,
        flags=flags,
    )


def _resident(shape):
    zeros = (0,) * len(shape)
    return pl.BlockSpec(shape, lambda *_: zeros, pipeline_mode=pl.Buffered(1))


def _rms(x, g):
    return x * lax.rsqrt(jnp.mean(x * x, axis=-1, keepdims=True) + NORM_EPS) * g


def _rope(t, cos, sin_signed, quarter):
    lane = lax.broadcasted_iota(jnp.int32, t.shape, 1)
    first = (lane % (2 * quarter)) < quarter
    partner = jnp.where(first, pltpu.roll(t, LANES - quarter, 1), pltpu.roll(t, quarter, 1))
    return t * cos + partner * sin_signed


def _sigmoid(x):
    return 1.0 / (1.0 + jnp.exp(-x))


def _ada_kernel(c_ref, w_ref, b_ref, o_ref):
    c = c_ref[...]
    s = (c * _sigmoid(c)).astype(BF16)
    o_ref[...] = jnp.dot(s, w_ref[...].astype(BF16), preferred_element_type=F32) + b_ref[...]


def _ada_call(cond, w_ada, b_ada):
    d, n = w_ada.shape
    return pl.pallas_call(
        _ada_kernel,
        out_shape=jax.ShapeDtypeStruct((ADA_ROWS, n), F32),
        grid=(n // TN_ADA,),
        in_specs=[
            pl.BlockSpec((ADA_ROWS, d), lambda j: (0, 0)),
            pl.BlockSpec((d, TN_ADA), lambda j: (0, j)),
            pl.BlockSpec((1, TN_ADA), lambda j: (0, j)),
        ],
        out_specs=pl.BlockSpec((ADA_ROWS, TN_ADA), lambda j: (0, j)),
        compiler_params=_params(1),
        name="ada",
    )(cond, w_ada, b_ada)


def _kvq_kernel(*refs, with_q, with_rope):
    it = iter(refs)
    x_ref, mod_ref, n1g_ref, w_ref, gkv_ref, wkv_ref, gkb_ref = (next(it) for _ in range(7))
    if with_q:
        gq_ref, wq_ref, gqb_ref = (next(it) for _ in range(3))
    if with_rope:
        ca_ref, sa_ref, cb_ref, sb_ref = (next(it) for _ in range(4))
    kn_ref, va_ref, kpe_ref, kb_ref, vb_ref = (next(it) for _ in range(5))
    if with_q:
        qa_ref, qb_ref = (next(it) for _ in range(2))

    mod = mod_ref[0]
    z = (_rms(x_ref[0], n1g_ref[...]) * (1.0 + mod[1:2]) + mod[0:1]).astype(BF16)
    proj = jnp.dot(z, w_ref[...], preferred_element_type=F32)

    if with_rope:
        ca, sa, cb, sb = ca_ref[...], sa_ref[...], cb_ref[...], sb_ref[...]

    ckv = _rms(proj[:, _C_KV:_K_B], gkv_ref[...]).astype(BF16)
    kv_up = jnp.dot(ckv, wkv_ref[...], preferred_element_type=F32)
    hn = MLA_HEADS * MLA_NOPE
    kn_ref[0] = kv_up[:, :hn].astype(BF16)
    va_ref[0] = kv_up[:, hn:].astype(BF16)

    kpe = proj[:, _K_PE:_KV_END]
    if with_rope:
        kpe = _rope(kpe, ca, sa, MLA_ROPE // 4)
    kpe_ref[0] = kpe.astype(BF16)

    for h in range(GQA_KV_HEADS):
        lo = _K_B + h * GQA_HEAD_DIM
        t = _rms(proj[:, lo:lo + GQA_HEAD_DIM], gkb_ref[...])
        if with_rope:
            t = _rope(t, cb, sb, GQA_HEAD_DIM // 4)
        kb_ref[0, :, h * GQA_HEAD_DIM:(h + 1) * GQA_HEAD_DIM] = t.astype(BF16)
    vb_ref[0] = proj[:, _V_B:_K_PE].astype(BF16)

    if with_q:
        scale_a = LOG2_E * float(MLA_NOPE + MLA_ROPE) ** -0.5
        scale_b = LOG2_E * float(GQA_HEAD_DIM) ** -0.5
        cq = _rms(proj[:, _C_Q:_Q_B], gq_ref[...]).astype(BF16)
        q_up = jnp.dot(cq, wq_ref[...], preferred_element_type=F32)
        for h in range(MLA_HEADS):
            lo = h * QA_HEAD_W
            qa_ref[0, :, lo:lo + LANES] = (q_up[:, lo:lo + LANES] * scale_a).astype(BF16)
            pe = q_up[:, lo + LANES:lo + QA_HEAD_W]
            if with_rope:
                pe = _rope(pe, ca, sa, MLA_ROPE // 4)
            qa_ref[0, :, lo + LANES:lo + QA_HEAD_W] = (pe * scale_a).astype(BF16)
        for h in range(GQA_HEADS):
            lo = _Q_B + h * GQA_HEAD_DIM
            t = _rms(proj[:, lo:lo + GQA_HEAD_DIM], gqb_ref[...])
            if with_rope:
                t = _rope(t, cb, sb, GQA_HEAD_DIM // 4)
            qb_ref[0, :, h * GQA_HEAD_DIM:(h + 1) * GQA_HEAD_DIM] = (t * scale_b).astype(BF16)


def _kvq_call(x, mod3, mod_row, n1g, w_small, gkv, wkv, gkb, q_parts, rope_tabs, tm):
    b, t, d = x.shape
    with_q = q_parts is not None
    with_rope = rope_tabs is not None
    nt = t // tm
    cols = w_small.shape[1]

    in_specs = [
        pl.BlockSpec((1, tm, d), lambda bi, i: (bi, i, 0)),
        pl.BlockSpec((1, 6, d), lambda bi, i: (mod_row(bi), 0, 0)),
        _resident((1, d)),
        _resident((d, cols)),
        _resident(gkv.shape),
        _resident(wkv.shape),
        _resident(gkb.shape),
    ]
    args = [x, mod3, n1g, w_small, gkv, wkv, gkb]
    if with_q:
        gq, wq, gqb = q_parts
        in_specs += [_resident(gq.shape), _resident(wq.shape), _resident(gqb.shape)]
        args += [gq, wq, gqb]
    if with_rope:
        in_specs += [pl.BlockSpec((tm, LANES), lambda bi, i: (i, 0))] * 4
        args += list(rope_tabs)

    def tok(width):
        return pl.BlockSpec((1, tm, width), lambda bi, i: (bi, i, 0))

    widths = [MLA_HEADS * MLA_NOPE, MLA_HEADS * MLA_V, LANES,
              GQA_KV_HEADS * GQA_HEAD_DIM, GQA_KV_HEADS * GQA_HEAD_DIM]
    if with_q:
        widths += [MLA_HEADS * QA_HEAD_W, GQA_HEADS * GQA_HEAD_DIM]
    return pl.pallas_call(
        functools.partial(_kvq_kernel, with_q=with_q, with_rope=with_rope),
        out_shape=[jax.ShapeDtypeStruct((b, t, w), BF16) for w in widths],
        grid=(b, nt),
        in_specs=in_specs,
        out_specs=[tok(w) for w in widths],
        compiler_params=_params(2),
        name="kvq_lat" if with_q else "kvq_ctx",
    )(*args)


def _rows(c, tq):
    if isinstance(c, int):
        return pl.ds(c * tq, tq)
    return pl.ds(pl.multiple_of(c * tq, tq), tq)


def _scores_t(k, q):
    return lax.dot_general(k, q, (((1,), (1,)), ((), ())), preferred_element_type=F32)


def _attn_pipeline(n_heads, n_chunks, load_q, k_of, vt_of, store_o, s_bufs, p_bufs, key_block):
    items = [(h, c) for h in range(n_heads) for c in range(n_chunks)]
    n_kb = s_bufs[0].shape[0] // key_block
    col_max = {}
    for t in range(-2, len(items)):
        qk_i, ex_i, pv_i = t + 2, t + 1, t
        do_qk, do_ex, do_pv = qk_i < len(items), 0 <= ex_i < len(items), pv_i >= 0
        if do_qk:
            k_ref, q = k_of(items[qk_i][0]), load_q(*items[qk_i])
        if do_pv:
            vt_ref = vt_of(items[pv_i][0])
        m_new = acc = None
        for kb in range(n_kb):
            rows = slice(kb * key_block, (kb + 1) * key_block)
            if do_qk:
                s_blk = _scores_t(k_ref[rows, :], q)
                s_bufs[qk_i % 2][rows, :] = s_blk
                bm = jnp.max(s_blk, axis=0, keepdims=True)
                m_new = bm if m_new is None else jnp.maximum(m_new, bm)
            if do_ex:
                s_blk = s_bufs[ex_i % 2][rows, :]
                p_bufs[ex_i % 2][rows, :] = jnp.exp2(s_blk - col_max[ex_i]).astype(BF16)
            if do_pv:
                d = jnp.dot(vt_ref[kb], p_bufs[pv_i % 2][rows, :], preferred_element_type=F32)
                acc = d if acc is None else acc + d
        if do_qk:
            col_max[qk_i] = m_new
        if do_pv:
            dv = acc.shape[0] - ONES_ROWS
            store_o(*items[pv_i], (acc[:dv] * (1.0 / acc[dv:dv + 1])).T)


def _fill_vt(vt_ref, v_lat, v_ctx):
    kb_w = vt_ref.shape[2]
    dv = vt_ref.shape[1] - ONES_ROWS
    blocks = [v_lat[i:i + kb_w] for i in range(0, v_lat.shape[0], kb_w)]
    blocks += [v_ctx[i:i + kb_w] for i in range(0, v_ctx.shape[0], kb_w)]
    for kb, blk in enumerate(blocks):
        vt_ref[kb, :dv, :] = blk.astype(F32).T.astype(BF16)
        vt_ref[kb, dv:, :] = jnp.ones((ONES_ROWS, kb_w), BF16)


def _attn_a_kernel(q_ref, knl_ref, kpl_ref, val_ref, knc_ref, kpc_ref, vac_ref, o_ref,
                   k_scr, vt_scr, s0, s1, p0, p1, *, tq, heads):
    n_lat = knl_ref.shape[1]
    for h in range(heads):
        cols = slice(h * LANES, (h + 1) * LANES)
        k_scr[h, :n_lat, :LANES] = knl_ref[0, :, cols]
        k_scr[h, :n_lat, LANES:] = kpl_ref[0]
        k_scr[h, n_lat:, :LANES] = knc_ref[0, :, cols]
        k_scr[h, n_lat:, LANES:] = kpc_ref[0]
        _fill_vt(vt_scr.at[h], val_ref[0, :, cols], vac_ref[0, :, cols])

    def load_q(h, c):
        return q_ref[0, _rows(c, tq), h * QA_HEAD_W:(h + 1) * QA_HEAD_W]

    def store_o(h, c, o):
        o_ref[0, _rows(c, tq), h * MLA_V:(h + 1) * MLA_V] = o.astype(BF16)

    _attn_pipeline(heads, q_ref.shape[1] // tq, load_q, lambda h: k_scr.at[h],
                   lambda h: vt_scr.at[h], store_o, (s0, s1), (p0, p1), KEY_BLOCK)


def _attn_a_call(qa, kn_l, kpe_l, va_l, kn_c, kpe_c, va_c):
    b, s, _ = qa.shape
    c = kn_c.shape[1]
    tq = min(TQ_ATTN, s)
    hs = HEADS_PER_STEP_A

    def heads(rows, width):
        return pl.BlockSpec((1, rows, hs * width), lambda bi, h: (bi, 0, h))

    def shared(rows):
        return pl.BlockSpec((1, rows, LANES), lambda bi, h: (bi, 0, 0))

    return pl.pallas_call(
        functools.partial(_attn_a_kernel, tq=tq, heads=hs),
        out_shape=jax.ShapeDtypeStruct((b, s, MLA_HEADS * MLA_V), BF16),
        grid=(b, MLA_HEADS // hs),
        in_specs=[heads(s, QA_HEAD_W), heads(s, MLA_NOPE), shared(s), heads(s, MLA_V),
                  heads(c, MLA_NOPE), shared(c), heads(c, MLA_V)],
        out_specs=heads(s, MLA_V),
        scratch_shapes=[pltpu.VMEM((hs, s + c, QA_HEAD_W), BF16),
                        pltpu.VMEM((hs, (s + c) // KEY_BLOCK, MLA_V + ONES_ROWS, KEY_BLOCK), BF16),
                        pltpu.VMEM((s + c, tq), F32), pltpu.VMEM((s + c, tq), F32),
                        pltpu.VMEM((s + c, tq), BF16), pltpu.VMEM((s + c, tq), BF16)],
        compiler_params=_params(2),
        name="attn_a",
    )(qa, kn_l, kpe_l, va_l, kn_c, kpe_c, va_c)


def _attn_b_kernel(q_ref, kl_ref, vl_ref, kc_ref, vc_ref, o_ref, k_scr, vt_scr, s0, s1, p0, p1,
                   *, tq):
    n_lat = kl_ref.shape[1]
    k_scr[:n_lat] = kl_ref[0]
    k_scr[n_lat:] = kc_ref[0]
    hd = GQA_HEAD_DIM
    _fill_vt(vt_scr, vl_ref[0], vc_ref[0])

    def load_q(h, c):
        return q_ref[0, _rows(c, tq), h * hd:(h + 1) * hd]

    def store_o(h, c, o):
        o_ref[0, _rows(c, tq), h * hd:(h + 1) * hd] = o.astype(BF16)

    _attn_pipeline(GQA_HEADS // GQA_KV_HEADS, q_ref.shape[1] // tq, load_q, lambda h: k_scr,
                   lambda h: vt_scr, store_o, (s0, s1), (p0, p1), KEY_BLOCK)


def _attn_b_call(qb, kb_l, vb_l, kb_c, vb_c):
    b, s, _ = qb.shape
    c = kb_c.shape[1]
    tq = min(TQ_ATTN, s)
    gw = (GQA_HEADS // GQA_KV_HEADS) * GQA_HEAD_DIM

    def kv(rows):
        return pl.BlockSpec((1, rows, GQA_HEAD_DIM), lambda bi, g: (bi, 0, g))

    qo = pl.BlockSpec((1, s, gw), lambda bi, g: (bi, 0, g))
    return pl.pallas_call(
        functools.partial(_attn_b_kernel, tq=tq),
        out_shape=jax.ShapeDtypeStruct((b, s, GQA_HEADS * GQA_HEAD_DIM), BF16),
        grid=(b, GQA_KV_HEADS),
        in_specs=[qo, kv(s), kv(s), kv(c), kv(c)],
        out_specs=qo,
        scratch_shapes=[pltpu.VMEM((s + c, GQA_HEAD_DIM), BF16),
                        pltpu.VMEM(((s + c) // KEY_BLOCK, GQA_HEAD_DIM + ONES_ROWS, KEY_BLOCK), BF16),
                        pltpu.VMEM((s + c, tq), F32), pltpu.VMEM((s + c, tq), F32),
                        pltpu.VMEM((s + c, tq), BF16), pltpu.VMEM((s + c, tq), BF16)],
        compiler_params=_params(2),
        name="attn_b",
    )(qb, kb_l, vb_l, kb_c, vb_c)


def _merge_kernel(x_ref, mod_ref, n1g_ref, oa_ref, ob_ref, wga_ref, wgb_ref, wba_ref, wbb_ref,
                  wo_ref, n2g_ref, x1_ref, z2_ref, z_scr, m_scr):
    j = pl.program_id(2)
    nj = pl.num_programs(2)
    mod = mod_ref[0]

    @pl.when(j == 0)
    def _():
        z_scr[...] = (_rms(x_ref[0], n1g_ref[...]) * (1.0 + mod[1:2]) + mod[0:1]).astype(BF16)

    z = z_scr[...]
    ga = _sigmoid(jnp.dot(z, wga_ref[...], preferred_element_type=F32))
    gb = _sigmoid(jnp.dot(z, wgb_ref[...], preferred_element_type=F32))
    ba = jnp.dot(oa_ref[0], wba_ref[...], preferred_element_type=F32)
    bb = jnp.dot(ob_ref[0], wbb_ref[...], preferred_element_type=F32)
    m_scr[j] = (ga * ba + gb * bb).astype(BF16)

    @pl.when(j == nj - 1)
    def _():
        tc = m_scr.shape[2]
        a = jnp.dot(m_scr[0], wo_ref[:tc, :], preferred_element_type=F32)
        for k in range(1, m_scr.shape[0]):
            a = a + jnp.dot(m_scr[k], wo_ref[k * tc:(k + 1) * tc, :], preferred_element_type=F32)
        x1 = x_ref[0] + mod[2:3] * a
        x1_ref[0] = x1
        z2_ref[0] = (_rms(x1, n2g_ref[...]) * (1.0 + mod[4:5]) + mod[3:4]).astype(BF16)


def _merge_call(x, mod3, n1g, oa, ob, wga, wgb, wba, wbb, wo, n2g):
    b, s, d = x.shape
    tm = min(TM_MERGE, s)
    tc = TC_MERGE
    nj = d // tc
    ka, kb = wba.shape[0], wbb.shape[0]

    def tok(width):
        return pl.BlockSpec((1, tm, width), lambda bi, i, j: (bi, i, 0))

    def wcol(rows):
        return pl.BlockSpec((rows, tc), lambda bi, i, j: (0, j))

    return pl.pallas_call(
        _merge_kernel,
        out_shape=[jax.ShapeDtypeStruct((b, s, d), F32), jax.ShapeDtypeStruct((b, s, d), BF16)],
        grid=(b, s // tm, nj),
        in_specs=[tok(d), pl.BlockSpec((1, 6, d), lambda bi, i, j: (bi, 0, 0)), _resident((1, d)),
                  tok(ka), tok(kb), wcol(d), wcol(d), wcol(ka), wcol(kb),
                  _resident((d, d)), _resident((1, d))],
        out_specs=[tok(d), tok(d)],
        scratch_shapes=[pltpu.VMEM((tm, d), BF16), pltpu.VMEM((nj, tm, tc), BF16)],
        compiler_params=_params(3),
        name="merge",
    )(x, mod3, n1g, oa, ob, wga, wgb, wba, wbb, wo, n2g)


def _ffn_kernel(z_ref, zp_ref, zn_ref, x1_ref, mod_ref, wa_ref, wb_ref, cwa_ref, cwb_ref,
                cba_ref, cbb_ref, wd_ref, fg_ref, o_ref, zext, acc):
    i = pl.program_id(1)
    f = pl.program_id(2)
    tm = z_ref.shape[1]
    halo = zp_ref.shape[1]
    n_ext = zext.shape[0]

    @pl.when(f == 0)
    def _():
        zext[:halo] = jnp.where(i > 0, zp_ref[0], jnp.zeros_like(zp_ref[0]))
        zext[halo:halo + tm] = z_ref[0]
        zext[halo + tm:] = jnp.where(i < pl.num_programs(1) - 1, zn_ref[0],
                                     jnp.zeros_like(zn_ref[0]))
        acc[...] = jnp.zeros_like(acc)

    def conv(u, cw_ref, cb_ref):
        cw = cw_ref[...]
        prev = pltpu.roll(u, 1, 0)[halo:halo + tm]
        nxt = pltpu.roll(u, n_ext - 1, 0)[halo:halo + tm]
        return cb_ref[...] + cw[0:1] * prev + cw[1:2] * u[halo:halo + tm] + cw[2:3] * nxt

    ze = zext[...]
    ua = conv(jnp.dot(ze, wa_ref[...], preferred_element_type=F32), cwa_ref, cba_ref)
    ub = conv(jnp.dot(ze, wb_ref[...], preferred_element_type=F32), cwb_ref, cbb_ref)
    h = (ua * _sigmoid(ua) * ub).astype(BF16)
    acc[...] += jnp.dot(h, wd_ref[...], preferred_element_type=F32)

    @pl.when(f == pl.num_programs(2) - 1)
    def _():
        x2 = x1_ref[0] + mod_ref[0][5:6] * acc[...]
        o_ref[0] = _rms(x2, fg_ref[...])


def _ffn_call(z2, x1, mod3, w_up, conv_w, conv_b, w_down, fg):
    b, s, d = x1.shape
    dff = w_down.shape[0]
    tm = min(TM_FFN, s)
    tf = TF_FFN
    nf = dff // tf
    halo = BF16_SUBLANES
    hb = tm // halo
    last_hb = s // halo - 1

    def tok():
        return pl.BlockSpec((1, tm, d), lambda bi, i, f: (bi, i, 0))

    prev = pl.BlockSpec((1, halo, d), lambda bi, i, f: (bi, jnp.maximum(i * hb - 1, 0), 0))
    nxt = pl.BlockSpec((1, halo, d), lambda bi, i, f: (bi, jnp.minimum((i + 1) * hb, last_hb), 0))

    def up_cols(rows, second):
        off = nf if second else 0
        return pl.BlockSpec((rows, tf), lambda bi, i, f: (0, f + off))

    return pl.pallas_call(
        _ffn_kernel,
        out_shape=jax.ShapeDtypeStruct((b, s, d), F32),
        grid=(b, s // tm, nf),
        in_specs=[tok(), prev, nxt, tok(),
                  pl.BlockSpec((1, 6, d), lambda bi, i, f: (bi, 0, 0)),
                  up_cols(d, False), up_cols(d, True),
                  up_cols(CONV_W, False), up_cols(CONV_W, True),
                  up_cols(1, False), up_cols(1, True),
                  pl.BlockSpec((tf, d), lambda bi, i, f: (f, 0)),
                  _resident((1, d))],
        out_specs=tok(),
        scratch_shapes=[pltpu.VMEM((tm + 2 * halo, d), BF16), pltpu.VMEM((tm, d), F32)],
        compiler_params=_params(3),
        name="ffn",
    )(z2, z2, z2, x1, mod3, w_up, w_up, conv_w, conv_w, conv_b, conv_b, w_down, fg)


def _rope_table(n_rows, rot_dim):
    row = jnp.repeat(jnp.arange(n_rows, dtype=F32), GRID_W)
    col = jnp.tile(jnp.arange(GRID_W, dtype=F32), n_rows)
    half = rot_dim // 2
    inv_freq = ROPE_THETA ** (-jnp.arange(0, half, 2, dtype=F32) / half)
    t = row.shape[0]
    quarter = rot_dim // 4
    ang = jnp.stack([row[:, None] * inv_freq, col[:, None] * inv_freq], axis=1)
    cos = jnp.broadcast_to(jnp.cos(ang)[:, :, None, :], (t, 2, 2, quarter))
    sin = jnp.sin(ang)
    sin = jnp.stack([-sin, sin], axis=2)
    pad = ((0, 0), (0, LANES - rot_dim))
    return jnp.pad(cos.reshape(t, rot_dim), pad), jnp.pad(sin.reshape(t, rot_dim), pad)


def kernel(x, c, ctx, c_ctx, w_ada, b_ada, norm1_g, w_in, mla_q_norm_g, w_q_up, mla_kv_norm_g,
           w_kv_up, gqa_q_norm_g, gqa_k_norm_g, w_br_a, w_br_b, w_out, norm2_g, w_up, conv_w,
           conv_b, w_down, final_norm_g):
    bsz, seq, d = x.shape
    n_ctx = ctx.shape[1]
    depth = w_in.shape[0]
    assert depth == 1 and seq % GRID_W == 0 and bsz < ADA_ROWS

    kv_lora, rope_a = MLA_KV_LORA, MLA_ROPE
    kb_w = GQA_KV_HEADS * GQA_HEAD_DIM
    kv_cols = kv_lora + rope_a + 2 * kb_w
    q_cols = MLA_Q_LORA + GQA_HEADS * GQA_HEAD_DIM

    wi = w_in[0]

    def cols(lo, hi):
        return wi[:, lo:hi].astype(BF16)

    c_kv = cols(0, kv_lora)
    k_pe = cols(kv_lora, kv_lora + rope_a)
    k_b = cols(kv_lora + rope_a, kv_lora + rope_a + kb_w)
    v_b = cols(kv_lora + rope_a + kb_w, kv_cols)
    q_part = cols(kv_cols, kv_cols + q_cols)
    w_small = jnp.concatenate(
        [c_kv, k_b, v_b, k_pe, jnp.zeros((d, LANES - rope_a), BF16), q_part], axis=1)
    w_ga = cols(kv_cols + q_cols, kv_cols + q_cols + d)
    w_gb = cols(kv_cols + q_cols + d, kv_cols + q_cols + 2 * d)

    wkv = w_kv_up[0].reshape(kv_lora, MLA_HEADS, 2, MLA_NOPE).transpose(0, 2, 1, 3)
    wkv = wkv.reshape(kv_lora, 2 * MLA_HEADS * MLA_NOPE).astype(BF16)
    wq = w_q_up[0].reshape(MLA_Q_LORA, MLA_HEADS, MLA_NOPE + MLA_ROPE)
    wq = jnp.pad(wq, ((0, 0), (0, 0), (0, QA_HEAD_W - MLA_NOPE - MLA_ROPE)))
    wq = wq.reshape(MLA_Q_LORA, MLA_HEADS * QA_HEAD_W).astype(BF16)

    rope_tabs = (*_rope_table(seq // GRID_W, MLA_ROPE), *_rope_table(seq // GRID_W, GQA_HEAD_DIM))

    cond = jnp.zeros((ADA_ROWS, d), F32).at[:bsz].set(c).at[bsz].set(c_ctx)
    mod3 = _ada_call(cond, w_ada[0], b_ada).reshape(ADA_ROWS, 6, d)

    n1g = norm1_g
    gkv, gkb = mla_kv_norm_g, gqa_k_norm_g
    gq, gqb = mla_q_norm_g, gqa_q_norm_g

    kn_c, va_c, kpe_c, kb_c, vb_c = _kvq_call(
        ctx, mod3, lambda bi: bsz, n1g, w_small[:, :_KV_END], gkv, wkv, gkb, None, None,
        min(TM_KVQ, n_ctx))
    kn_l, va_l, kpe_l, kb_l, vb_l, qa, qb = _kvq_call(
        x, mod3, lambda bi: bi, n1g, w_small, gkv, wkv, gkb, (gq, wq, gqb), rope_tabs,
        min(TM_KVQ, seq))

    oa = _attn_a_call(qa, kn_l, kpe_l, va_l, kn_c, kpe_c, va_c)
    ob = _attn_b_call(qb, kb_l, vb_l, kb_c, vb_c)

    x1, z2 = _merge_call(x, mod3, n1g, oa, ob, w_ga, w_gb, w_br_a[0].astype(BF16),
                         w_br_b[0].astype(BF16), w_out[0].astype(BF16), norm2_g)
    return _ffn_call(z2, x1, mod3, w_up[0].astype(BF16), conv_w[0], conv_b, w_down[0].astype(BF16),
                     final_norm_g.reshape(1, d))
```
